```python
import math
import jax, jax.numpy as jnp
from jax import lax
import numpy as np

D_MODEL = 1024
BATCH = 16
SEQ = 4096
DEPTH = 2

PLE_DIM = 256
EPS = 1e-6
ROPE_THETA = 500000.0
RET_THETA = 10000.0
BLOCK_Q = 128
CHUNK = 128
N_BRANCH = 4

LRU_WIDTH = D_MODEL
LRU_BLOCKS = 16
LRU_BLOCK = LRU_WIDTH // LRU_BLOCKS
CONV_WIDTH = 4
LRU_C = 8.0
DIFF_HEADS = 8
DIFF_DH = D_MODEL // (2 * DIFF_HEADS)
DIFF_ROT = DIFF_DH // 4
MLA_HEADS = 8
MLA_NOPE = 128
MLA_ROPE = 64
MLA_V = 128
MLA_Q_LORA = 3 * D_MODEL // 8
MLA_KV_LORA = D_MODEL // 4
RET_HEADS = 8
RET_DK = 64
RET_DV = 128
D_FF = 4 * D_MODEL

IN_SIZES = (LRU_WIDTH, LRU_WIDTH,
            DIFF_HEADS * 2 * DIFF_DH, DIFF_HEADS * 2 * DIFF_DH, DIFF_HEADS * 2 * DIFF_DH,
            MLA_Q_LORA, MLA_KV_LORA, MLA_ROPE,
            RET_HEADS * RET_DK, RET_HEADS * RET_DK, RET_HEADS * RET_DV, RET_HEADS * RET_DV,
            N_BRANCH * D_MODEL)
IN_OFFSETS = tuple(sum(IN_SIZES[:j]) for j in range(1, len(IN_SIZES)))
N_IN = sum(IN_SIZES)

kernel_name = 'hybrid_gated_bidir_encoder'


def _rms_norm(x, g):
    x32 = x.astype(jnp.float32)
    y = x32 * lax.rsqrt(jnp.mean(x32 * x32, axis=-1, keepdims=True) + EPS)
    return (y * g.astype(jnp.float32)).astype(x.dtype)


def _head_layer_norm(x, g):
    x32 = x.astype(jnp.float32)
    xc = x32 - jnp.mean(x32, axis=-1, keepdims=True)
    var = jnp.mean(xc * xc, axis=-1, keepdims=True)
    return (xc * lax.rsqrt(var + EPS) * g.astype(jnp.float32)).astype(x.dtype)


def _rope_tables(seq, rot_dim, theta, dtype):
    pos = jnp.arange(seq, dtype=jnp.float32)
    inv_freq = theta ** (-jnp.arange(0, rot_dim, 2, dtype=jnp.float32) / rot_dim)
    ang = pos[:, None] * inv_freq[None, :]
    return jnp.cos(ang).astype(dtype), jnp.sin(ang).astype(dtype)


def _apply_rope(x, cos, sin):
    x1, x2 = jnp.split(x, 2, axis=-1)
    return jnp.concatenate([x1 * cos - x2 * sin, x2 * cos + x1 * sin], axis=-1)


def _partial_rope(x, cos, sin, rot):
    return jnp.concatenate([_apply_rope(x[..., :rot], cos, sin), x[..., rot:]], axis=-1)


def _blocked_queries(fn, qs):
    def to_blocks(t):
        b, h, s, d = t.shape
        return t.reshape(b, h, s // BLOCK_Q, BLOCK_Q, d).transpose(2, 0, 1, 3, 4)
    out = lax.map(fn, tuple(to_blocks(t) for t in qs))
    nb, b, h, qb, e = out.shape
    return out.transpose(1, 2, 0, 3, 4).reshape(b, h, nb * qb, e)


def _linear_recurrence_combine(left, right):
    a_l, b_l = left
    a_r, b_r = right
    return a_l * a_r, a_r * b_l + b_r


def _rglru_mixer(xa, ga, conv_w, conv_b, wa, ba, wx, bx, lam):
    b, s, w = xa.shape
    left = CONV_WIDTH // 2
    xp = jnp.pad(xa, ((0, 0), (left, CONV_WIDTH - 1 - left), (0, 0)))
    xc = conv_b
    for t in range(CONV_WIDTH):
        xc = xc + xp[:, t:t + s] * conv_w[t]
    xblk = xc.reshape(b, s, LRU_BLOCKS, LRU_BLOCK)
    h_sum = jnp.zeros_like(xc)
    for d in range(2):
        r = jax.nn.sigmoid(jnp.einsum('bsnc,ncd->bsnd', xblk, wa[d]).reshape(b, s, w) + ba[d])
        i = jax.nn.sigmoid(jnp.einsum('bsnc,ncd->bsnd', xblk, wx[d]).reshape(b, s, w) + bx[d])
        log_a = -LRU_C * r * jax.nn.softplus(-lam[d])
        a = jnp.exp(log_a)
        u = jnp.sqrt(-jnp.expm1(2.0 * log_a)) * (i * xc)
        _, h = lax.associative_scan(_linear_recurrence_combine, (a, u), axis=1, reverse=(d == 1))
        h_sum = h_sum + h
    return h_sum * jax.nn.gelu(ga)


def _diff_attention(q, k, v, q_g, k_g, lam_p, sub_g, lam_init, cos, sin):
    b, s, _ = q.shape
    q = _rms_norm(q.reshape(b, s, DIFF_HEADS, 2, DIFF_DH), q_g).transpose(0, 2, 3, 1, 4)
    k = _rms_norm(k.reshape(b, s, DIFF_HEADS, 2, DIFF_DH), k_g).transpose(0, 2, 3, 1, 4)
    v = v.reshape(b, s, DIFF_HEADS, 2 * DIFF_DH).transpose(0, 2, 1, 3)
    q = _partial_rope(q, cos, sin, DIFF_ROT)
    k = _partial_rope(k, cos, sin, DIFF_ROT)
    q1, q2 = q[:, :, 0], q[:, :, 1]
    k1, k2 = k[:, :, 0], k[:, :, 1]
    lp = lam_p.astype(jnp.float32)
    lam = jnp.exp(jnp.sum(lp[0] * lp[1])) - jnp.exp(jnp.sum(lp[2] * lp[3])) + lam_init
    scale = DIFF_DH ** -0.5

    def block(qb):
        q1b, q2b = qb
        a1 = jax.nn.softmax((jnp.einsum('bhqd,bhkd->bhqk', q1b, k1) * scale).astype(jnp.float32), axis=-1)
        a2 = jax.nn.softmax((jnp.einsum('bhqd,bhkd->bhqk', q2b, k2) * scale).astype(jnp.float32), axis=-1)
        return jnp.einsum('bhqk,bhke->bhqe', (a1 - lam * a2).astype(v.dtype), v)

    o = _blocked_queries(block, (q1, q2))
    o = _rms_norm(o, sub_g) * (1.0 - lam_init)
    return o.transpose(0, 2, 1, 3).reshape(b, s, DIFF_HEADS * 2 * DIFF_DH)


def _mla(cq, ckv, kpe, qa_g, wuq, kva_g, wukv, q_g, k_g, cos, sin):
    b, s, _ = cq.shape
    q = (_rms_norm(cq, qa_g) @ wuq).reshape(b, s, MLA_HEADS, MLA_NOPE + MLA_ROPE)
    kv = (_rms_norm(ckv, kva_g) @ wukv).reshape(b, s, MLA_HEADS, MLA_NOPE + MLA_V)
    k_nope, v = kv[..., :MLA_NOPE], kv[..., MLA_NOPE:]
    k = jnp.concatenate([k_nope, jnp.broadcast_to(kpe[:, :, None, :], (b, s, MLA_HEADS, MLA_ROPE))], axis=-1)
    q = _rms_norm(q, q_g).transpose(0, 2, 1, 3)
    k = _rms_norm(k, k_g).transpose(0, 2, 1, 3)
    v = v.transpose(0, 2, 1, 3)
    q = jnp.concatenate([q[..., :MLA_NOPE], _apply_rope(q[..., MLA_NOPE:], cos, sin)], axis=-1)
    k = jnp.concatenate([k[..., :MLA_NOPE], _apply_rope(k[..., MLA_NOPE:], cos, sin)], axis=-1)
    scale = (MLA_NOPE + MLA_ROPE) ** -0.5

    def block(qb):
        (q_b,) = qb
        a = jax.nn.softmax((jnp.einsum('bhqd,bhkd->bhqk', q_b, k) * scale).astype(jnp.float32), axis=-1)
        return jnp.einsum('bhqk,bhke->bhqe', a.astype(v.dtype), v)

    o = _blocked_queries(block, (q,))
    return o.transpose(0, 2, 1, 3).reshape(b, s, MLA_HEADS * MLA_V)


def _retention(q, k, v, g, gn_g, cos, sin):
    b, s, _ = q.shape
    nc = s // CHUNK
    q = _apply_rope(q.reshape(b, s, RET_HEADS, RET_DK).transpose(0, 2, 1, 3), cos, sin)
    k = _apply_rope(k.reshape(b, s, RET_HEADS, RET_DK).transpose(0, 2, 1, 3), cos, sin) * (RET_DK ** -0.5)
    v = v.reshape(b, s, RET_HEADS, RET_DV).transpose(0, 2, 1, 3)
    qc = q.reshape(b, RET_HEADS, nc, CHUNK, RET_DK)
    kc = k.reshape(b, RET_HEADS, nc, CHUNK, RET_DK)
    vc = v.reshape(b, RET_HEADS, nc, CHUNK, RET_DV)
    log_g = jnp.log1p(-jnp.exp2(-5.0 - jnp.arange(RET_HEADS, dtype=jnp.float32)))
    idx = jnp.arange(CHUNK, dtype=jnp.float32)
    dt = q.dtype
    d_intra = jnp.exp(log_g[:, None, None] * jnp.abs(idx[:, None] - idx[None, :])).astype(dt)
    d_tail = jnp.exp(log_g[:, None] * (CHUNK - 1.0 - idx)[None, :]).astype(dt)
    d_head = jnp.exp(log_g[:, None] * idx[None, :]).astype(dt)
    d_qf = jnp.exp(log_g[:, None] * (idx + 1.0)[None, :]).astype(dt)
    d_qb = jnp.exp(log_g[:, None] * (CHUNK - idx)[None, :]).astype(dt)
    intra = jnp.einsum('bhnij,bhnje->bhnie',
                       jnp.einsum('bhnid,bhnjd->bhnij', qc, kc) * d_intra[:, None], vc)
    kv_fwd = jnp.einsum('bhncd,bhnce->nbhde', kc * d_tail[:, None, :, None], vc)
    kv_bwd = jnp.einsum('bhncd,bhnce->nbhde', kc * d_head[:, None, :, None], vc)
    d_chunk = jnp.exp(log_g * CHUNK).astype(kv_fwd.dtype)[None, :, None, None]

    def step(carry, kv_c):
        return carry * d_chunk + kv_c, carry

    init = jnp.zeros((b, RET_HEADS, RET_DK, RET_DV), kv_fwd.dtype)
    _, past = lax.scan(step, init, kv_fwd)
    _, future = lax.scan(step, init, kv_bwd, reverse=True)
    cross = (jnp.einsum('bhnid,nbhde->bhnie', qc * d_qf[:, None, :, None], past)
             + jnp.einsum('bhnid,nbhde->bhnie', qc * d_qb[:, None, :, None], future))
    o = (intra + cross).reshape(b, RET_HEADS, s, RET_DV)
    o = _head_layer_norm(o, gn_g).transpose(0, 2, 1, 3).reshape(b, s, RET_HEADS * RET_DV)
    return jax.nn.silu(g) * o


def setup_inputs(seed: int = 0) -> dict:
    key = jax.random.key(seed)
    ks = jax.random.split(key, 34)
    L = DEPTH

    def nrm(k, shape, scale):
        return jax.random.normal(k, shape, jnp.float32) * scale

    def gain(k, shape):
        return 1.0 + 0.1 * jax.random.normal(k, shape, jnp.float32)

    a_c = jax.random.uniform(ks[11], (L, 2, LRU_WIDTH), jnp.float32, minval=0.9, maxval=0.999)
    a_base = a_c ** (1.0 / LRU_C)
    lru_lambda = jnp.log(a_base) - jnp.log1p(-a_base)
    return {
        'x': nrm(ks[0], (BATCH, SEQ, D_MODEL), 1.0),
        'p': nrm(ks[1], (DEPTH, BATCH, SEQ, PLE_DIM), 1.0),
        'norm1_g': gain(ks[2], (L, D_MODEL)),
        'w_in': nrm(ks[3], (L, D_MODEL, N_IN), D_MODEL ** -0.5),
        'gate_b': nrm(ks[4], (L, N_BRANCH, D_MODEL), 0.1),
        'conv_w': nrm(ks[5], (L, CONV_WIDTH, LRU_WIDTH), CONV_WIDTH ** -0.5),
        'conv_b': nrm(ks[6], (L, LRU_WIDTH), 0.01),
        'lru_wa': nrm(ks[7], (L, 2, LRU_BLOCKS, LRU_BLOCK, LRU_BLOCK), LRU_BLOCK ** -0.5),
        'lru_ba': nrm(ks[8], (L, 2, LRU_WIDTH), 0.1),
        'lru_wx': nrm(ks[9], (L, 2, LRU_BLOCKS, LRU_BLOCK, LRU_BLOCK), LRU_BLOCK ** -0.5),
        'lru_bx': nrm(ks[10], (L, 2, LRU_WIDTH), 0.1),
        'lru_lambda': lru_lambda,
        'diff_q_g': gain(ks[12], (L, DIFF_DH)),
        'diff_k_g': gain(ks[13], (L, DIFF_DH)),
        'diff_lam': nrm(ks[14], (L, 4, DIFF_DH), 0.1),
        'diff_sub_g': gain(ks[15], (L, 2 * DIFF_DH)),
        'mla_qa_g': gain(ks[16], (L, MLA_Q_LORA)),
        'mla_wuq': nrm(ks[17], (L, MLA_Q_LORA, MLA_HEADS * (MLA_NOPE + MLA_ROPE)), MLA_Q_LORA ** -0.5),
        'mla_kva_g': gain(ks[18], (L, MLA_KV_LORA)),
        'mla_wukv': nrm(ks[19], (L, MLA_KV_LORA, MLA_HEADS * (MLA_NOPE + MLA_V)), MLA_KV_LORA ** -0.5),
        'mla_q_g': gain(ks[20], (L, MLA_NOPE + MLA_ROPE)),
        'mla_k_g': gain(ks[21], (L, MLA_NOPE + MLA_ROPE)),
        'ret_gn_g': gain(ks[22], (L, RET_DV)),
        'w_br_a': nrm(ks[23], (L, LRU_WIDTH, D_MODEL), LRU_WIDTH ** -0.5),
        'w_br_b': nrm(ks[24], (L, DIFF_HEADS * 2 * DIFF_DH, D_MODEL), (DIFF_HEADS * 2 * DIFF_DH) ** -0.5),
        'w_br_c': nrm(ks[25], (L, MLA_HEADS * MLA_V, D_MODEL), (MLA_HEADS * MLA_V) ** -0.5),
        'w_br_d': nrm(ks[26], (L, RET_HEADS * RET_DV, D_MODEL), (RET_HEADS * RET_DV) ** -0.5),
        'w_out': nrm(ks[27], (L, D_MODEL, D_MODEL), D_MODEL ** -0.5),
        'norm2_g': gain(ks[28], (L, D_MODEL)),
        'w_ff1': nrm(ks[29], (L, D_MODEL, D_FF), D_MODEL ** -0.5),
        'w_ff2': nrm(ks[30], (L, D_FF, D_MODEL), D_FF ** -0.5),
        'norm3_g': gain(ks[31], (L, D_MODEL)),
        'w_ple_gate': nrm(ks[32], (L, D_MODEL, D_MODEL), D_MODEL ** -0.5),
        'w_ple_proj': nrm(ks[33], (L, PLE_DIM, D_MODEL), PLE_DIM ** -0.5),
    }


def reference(x, p, norm1_g, w_in, gate_b, conv_w, conv_b, lru_wa, lru_ba, lru_wx, lru_bx, lru_lambda,
              diff_q_g, diff_k_g, diff_lam, diff_sub_g, mla_qa_g, mla_wuq, mla_kva_g, mla_wukv,
              mla_q_g, mla_k_g, ret_gn_g, w_br_a, w_br_b, w_br_c, w_br_d, w_out, norm2_g,
              w_ff1, w_ff2, norm3_g, w_ple_gate, w_ple_proj):
    b, s, _ = x.shape
    cos_b, sin_b = _rope_tables(s, DIFF_ROT, ROPE_THETA, x.dtype)
    cos_c, sin_c = _rope_tables(s, MLA_ROPE, ROPE_THETA, x.dtype)
    cos_d, sin_d = _rope_tables(s, RET_DK, RET_THETA, x.dtype)
    for i in range(DEPTH):
        xn = _rms_norm(x, norm1_g[i])
        (a_x, a_g, b_q, b_k, b_v, c_q, c_kv, c_kpe,
         d_q, d_k, d_v, d_g, gate_logits) = jnp.split(xn @ w_in[i], IN_OFFSETS, axis=-1)
        y_a = _rglru_mixer(a_x, a_g, conv_w[i], conv_b[i], lru_wa[i], lru_ba[i], lru_wx[i], lru_bx[i],
                           lru_lambda[i])
        lam_init = 0.8 - 0.6 * math.exp(-0.3 * i)
        y_b = _diff_attention(b_q, b_k, b_v, diff_q_g[i], diff_k_g[i], diff_lam[i], diff_sub_g[i],
                              lam_init, cos_b, sin_b)
        y_c = _mla(c_q, c_kv, c_kpe, mla_qa_g[i], mla_wuq[i], mla_kva_g[i], mla_wukv[i],
                   mla_q_g[i], mla_k_g[i], cos_c, sin_c)
        y_d = _retention(d_q, d_k, d_v, d_g, ret_gn_g[i], cos_d, sin_d)
        gates = jax.nn.sigmoid(gate_logits.reshape(b, s, N_BRANCH, D_MODEL) + gate_b[i])
        merged = (gates[:, :, 0] * (y_a @ w_br_a[i]) + gates[:, :, 1] * (y_b @ w_br_b[i])
                  + gates[:, :, 2] * (y_c @ w_br_c[i]) + gates[:, :, 3] * (y_d @ w_br_d[i]))
        x = x + merged @ w_out[i]
        h = _rms_norm(x, norm2_g[i])
        x = x + jnp.square(jax.nn.relu(h @ w_ff1[i])) @ w_ff2[i]
        ple_gate = jax.nn.sigmoid(_rms_norm(x, norm3_g[i]) @ w_ple_gate[i])
        x = x + ple_gate * (p[i] @ w_ple_proj[i])
    return x
```

```python
import functools
import math

import jax
import jax.numpy as jnp
from jax import lax
from jax.experimental import pallas as pl
from jax.experimental.pallas import tpu as pltpu

F32 = jnp.float32
BF16 = jnp.bfloat16

EPS = 1e-6
ROPE_THETA = 500000.0
RET_THETA = 10000.0
LRU_C = 8.0
LOG2E = 1.4426950408889634

LANES = 128
MXU_DIM = 256
VMEM_LIMIT = 56 * 1024 * 1024

D_MODEL = 1024
N_BRANCH = 4
LRU_BLOCKS = 16
CONV_WIDTH = 4
DIFF_HEADS = 8
DIFF_DH = 64
DIFF_ROT = 16
MLA_HEADS = 8
MLA_NOPE = 128
MLA_ROPE = 64
MLA_V = 128
MLA_Q_LORA = 384
MLA_KV_LORA = 256
MLA_HEAD_PAD = 256
RET_HEADS = 8
RET_DK = 64
RET_DV = 128
RET_CHUNK = 256
N_SEG = 8
SEG_PAD = 4

OFF_GATE = 0
OFF_AX = 4096
OFF_AG = 5120
OFF_BQ = 6144
OFF_BK = 7168
OFF_BV = 8192
OFF_DV = 9216
OFF_DG = 10240
OFF_DQ = 11264
OFF_DK = 11776
OFF_C = 12288
C_WIDTH = 768
N_PROJ = OFF_C + C_WIDTH


def _cparams(sem):
    return pltpu.CompilerParams(dimension_semantics=sem, vmem_limit_bytes=VMEM_LIMIT)


def _sigmoid(x):
    return 1.0 / (1.0 + jnp.exp(-x))


def _rms(x, g):
    return x * lax.rsqrt(jnp.mean(x * x, axis=-1, keepdims=True) + EPS) * g


def _dot(a, b):
    return jnp.dot(a, b, preferred_element_type=F32)


def _dot_nt(a, b):
    return lax.dot_general(a, b, (((1,), (1,)), ((), ())), preferred_element_type=F32)


def _dot_tn(a, b):
    return lax.dot_general(a, b, (((0,), (0,)), ((), ())), preferred_element_type=F32)


def _inproj_kernel(x_ref, g_ref, w_ref, o_ref, xn_ref):
    @pl.when(pl.program_id(1) == 0)
    def _():
        xn_ref[...] = _rms(x_ref[...], g_ref[...]).astype(BF16)

    o_ref[...] = _dot(xn_ref[...], w_ref[...]).astype(o_ref.dtype)


def _inproj(x2, g, w, tm, tn):
    m, k = x2.shape
    n = w.shape[1]
    return pl.pallas_call(
        _inproj_kernel,
        grid=(m // tm, n // tn),
        in_specs=[pl.BlockSpec((tm, k), lambda i, j: (i, 0)),
                  pl.BlockSpec((1, k), lambda i, j: (0, 0)),
                  pl.BlockSpec((k, tn), lambda i, j: (0, j))],
        out_specs=pl.BlockSpec((tm, tn), lambda i, j: (i, j)),
        out_shape=jax.ShapeDtypeStruct((m, n), BF16),
        scratch_shapes=[pltpu.VMEM((tm, k), BF16)],
        compiler_params=_cparams(("parallel", "arbitrary")),
        name="inproj",
    )(x2, g, w)


def _gelu_tanh(x):
    return 0.5 * x * (1.0 + jnp.tanh(0.7978845608028654 * (x + 0.044715 * x * x * x)))


def _rglru_kernel(x_ref, ga_ref, cw_ref, cb_ref, wd_ref, bd_ref, lam_ref, o_ref,
                  xpad, a0, u0, a1, u1, *, seq):
    w = MXU_DIM
    nslab = w // LANES
    seg = seq // N_SEG
    pitch = seg + SEG_PAD
    left = CONV_WIDTH // 2

    xpad[0:8, :] = jnp.zeros((8, w), F32)
    xpad[seq + 8:seq + 16, :] = jnp.zeros((8, w), F32)
    for s in range(N_SEG):
        xpad[8 + s * seg:8 + (s + 1) * seg, :] = x_ref[s * seg:(s + 1) * seg, :].astype(F32)

    lam = lam_ref[...]
    nlam = -lam
    nls = -LRU_C * (jnp.maximum(nlam, 0.0) + jnp.log(1.0 + jnp.exp(-jnp.abs(nlam))))

    a_scr = (a0, a1)
    u_scr = (u0, u1)
    for s in range(N_SEG):
        base = 8 + s * seg - left
        xc = cb_ref[...]
        for t in range(CONV_WIDTH):
            xc = xc + xpad[base + t:base + t + seg, :] * cw_ref[t:t + 1, :]
        z = _dot(xc.astype(BF16), wd_ref[0]) + bd_ref[0]
        for d in range(2):
            r = _sigmoid(z[:, (2 * d) * w:(2 * d + 1) * w])
            gi = _sigmoid(z[:, (2 * d + 1) * w:(2 * d + 2) * w])
            log_a = nls[d:d + 1, :] * r
            a = jnp.exp(log_a)
            u = jnp.sqrt(1.0 - jnp.exp(2.0 * log_a)) * (gi * xc)
            for sl in range(nslab):
                a_scr[d][sl, s * pitch:s * pitch + seg, :] = a[:, sl * LANES:(sl + 1) * LANES]
                u_scr[d][sl, s * pitch:s * pitch + seg, :] = u[:, sl * LANES:(sl + 1) * LANES]

    def rows(t):
        return pl.ds(t, N_SEG, stride=pitch)

    def scan_body(t, carry):
        new = []
        tb = seg - 1 - t
        for sl in range(nslab):
            hf, pf, hb, pb = carry[sl]
            af = a0[sl, rows(t), :]
            hf = af * hf + u0[sl, rows(t), :]
            pf = pf * af
            u0[sl, rows(t), :] = hf
            a0[sl, rows(t), :] = pf
            ab = a1[sl, rows(tb), :]
            hb = ab * hb + u1[sl, rows(tb), :]
            pb = pb * ab
            u1[sl, rows(tb), :] = hb
            a1[sl, rows(tb), :] = pb
            new.append((hf, pf, hb, pb))
        return tuple(new)

    zero = jnp.zeros((N_SEG, LANES), F32)
    one = jnp.ones((N_SEG, LANES), F32)
    fin = lax.fori_loop(0, seg, scan_body, tuple((zero, one, zero, one) for _ in range(nslab)))

    cf, cb = [], []
    for sl in range(nslab):
        hf, pf, hb, pb = fin[sl]
        c = jnp.zeros((1, LANES), F32)
        cfr = [c]
        for s in range(1, N_SEG):
            c = pf[s - 1:s, :] * c + hf[s - 1:s, :]
            cfr.append(c)
        cf.append(jnp.concatenate(cfr, axis=0))
        c = jnp.zeros((1, LANES), F32)
        cbr = [c]
        for s in range(N_SEG - 2, -1, -1):
            c = pb[s + 1:s + 2, :] * c + hb[s + 1:s + 2, :]
            cbr.append(c)
        cb.append(jnp.concatenate(cbr[::-1], axis=0))

    def fix_body(t, carry):
        for sl in range(nslab):
            hf = u0[sl, rows(t), :] + a0[sl, rows(t), :] * cf[sl]
            hb = u1[sl, rows(t), :] + a1[sl, rows(t), :] * cb[sl]
            u0[sl, rows(t), :] = hf + hb
        return carry

    lax.fori_loop(0, seg, fix_body, 0)

    for s in range(N_SEG):
        g = _gelu_tanh(ga_ref[s * seg:(s + 1) * seg, :].astype(F32))
        for sl in range(nslab):
            h = u0[sl, s * pitch:s * pitch + seg, :]
            o_ref[s * seg:(s + 1) * seg, sl * LANES:(sl + 1) * LANES] = (
                h * g[:, sl * LANES:(sl + 1) * LANES]).astype(o_ref.dtype)


def _rglru(proj, cw, cb, wd, bd, lam, batch, seq):
    w = MXU_DIM
    nt = D_MODEL // w
    nslab = w // LANES
    rows = N_SEG * (seq // N_SEG + SEG_PAD)
    scan_scratch = pltpu.VMEM((nslab, rows, LANES), F32)
    return pl.pallas_call(
        functools.partial(_rglru_kernel, seq=seq),
        grid=(batch, nt),
        in_specs=[pl.BlockSpec((seq, w), lambda b, c: (b, OFF_AX // w + c)),
                  pl.BlockSpec((seq, w), lambda b, c: (b, OFF_AG // w + c)),
                  pl.BlockSpec((CONV_WIDTH, w), lambda b, c: (0, c)),
                  pl.BlockSpec((1, w), lambda b, c: (0, c)),
                  pl.BlockSpec((1, w, 4 * w), lambda b, c: (c, 0, 0)),
                  pl.BlockSpec((1, 1, 4 * w), lambda b, c: (c, 0, 0)),
                  pl.BlockSpec((2, w), lambda b, c: (0, c))],
        out_specs=pl.BlockSpec((seq, w), lambda b, c: (b, c)),
        out_shape=jax.ShapeDtypeStruct((batch * seq, D_MODEL), BF16),
        scratch_shapes=[pltpu.VMEM((seq + 16, w), F32),
                        scan_scratch, scan_scratch, scan_scratch, scan_scratch],
        compiler_params=_cparams(("parallel", "parallel")),
        name="rglru",
    )(proj, proj, cw, cb, wd, bd, lam)


def _rope_block(x, c, sa, sb, half):
    return (x * c + pltpu.roll(x, LANES - half, axis=1) * sa
            + pltpu.roll(x, half, axis=1) * sb)


def _rope_lane_tables(seq, rot, period, theta):
    half = rot // 2
    pos = jnp.arange(seq, dtype=F32)
    inv_freq = theta ** (-jnp.arange(0, rot, 2, dtype=F32) / rot)
    ang = pos[:, None] * inv_freq[None, :]
    cos, sin = jnp.cos(ang), jnp.sin(ang)
    pad = period - rot
    c1 = jnp.concatenate([cos, cos, jnp.ones((seq, pad), F32)], axis=1)
    s1 = jnp.concatenate([-sin, jnp.zeros((seq, half + pad), F32)], axis=1)
    s2 = jnp.concatenate([jnp.zeros((seq, half), F32), sin, jnp.zeros((seq, pad), F32)], axis=1)
    rep = LANES // period
    return jnp.tile(c1, (1, rep)), jnp.tile(s1, (1, rep)), jnp.tile(s2, (1, rep))


def _diff_prep_kernel(q_ref, k_ref, gq_ref, gk_ref, c_ref, sa_ref, sb_ref, gm_ref,
                      qo_ref, ko_ref, *, qscale):
    c, sa, sb = c_ref[...], sa_ref[...], sb_ref[...]
    gm = gm_ref[...]
    for src, g_ref, dst, scale in ((q_ref, gq_ref, qo_ref, qscale), (k_ref, gk_ref, ko_ref, 1.0)):
        for blk in range(D_MODEL // MXU_DIM):
            x = src[:, blk * MXU_DIM:(blk + 1) * MXU_DIM].astype(F32)
            ms = _dot((x * x).astype(BF16), gm)
            xn = x * lax.rsqrt(ms + EPS) * g_ref[...]
            for h in range(MXU_DIM // LANES):
                xb = _rope_block(xn[:, h * LANES:(h + 1) * LANES], c, sa, sb, DIFF_ROT // 2)
                lo = blk * MXU_DIM + h * LANES
                dst[:, lo:lo + LANES] = (xb * scale).astype(dst.dtype)


def _diff_prep(proj, gq, gk, tabs, gm, seq, tm, qscale):
    m = proj.shape[0]
    nrow = seq // tm
    tab_spec = pl.BlockSpec((tm, LANES), lambda i: (i % nrow, 0))
    return pl.pallas_call(
        functools.partial(_diff_prep_kernel, qscale=qscale),
        grid=(m // tm,),
        in_specs=[pl.BlockSpec((tm, D_MODEL), lambda i: (i, OFF_BQ // D_MODEL)),
                  pl.BlockSpec((tm, D_MODEL), lambda i: (i, OFF_BK // D_MODEL)),
                  pl.BlockSpec((1, MXU_DIM), lambda i: (0, 0)),
                  pl.BlockSpec((1, MXU_DIM), lambda i: (0, 0)),
                  tab_spec, tab_spec, tab_spec,
                  pl.BlockSpec((MXU_DIM, MXU_DIM), lambda i: (0, 0))],
        out_specs=[pl.BlockSpec((tm, D_MODEL), lambda i: (i, 0)),
                   pl.BlockSpec((tm, D_MODEL), lambda i: (i, 0))],
        out_shape=[jax.ShapeDtypeStruct((m, D_MODEL), BF16),
                   jax.ShapeDtypeStruct((m, D_MODEL), BF16)],
        compiler_params=_cparams(("parallel",)),
        name="diff_prep",
    )(proj, proj, gq, gk, *tabs, gm)


def _mla_prep_kernel(c_ref, qag_ref, wuq_ref, kvag_ref, wukv_ref, gq_ref, gk_ref,
                     c_tab, sa_tab, sb_tab, qo_ref, ko_ref, vo_ref, *, qscale):
    dh = MLA_NOPE + MLA_ROPE
    half = MLA_ROPE // 2
    ct, sat, sbt = c_tab[...], sa_tab[...], sb_tab[...]
    cq = c_ref[:, 0:MLA_Q_LORA].astype(F32)
    ckv = c_ref[:, MLA_Q_LORA:MLA_Q_LORA + MLA_KV_LORA].astype(F32)
    kpe = c_ref[:, MLA_Q_LORA + MLA_KV_LORA:C_WIDTH].astype(F32)
    q = _dot(_rms(cq, qag_ref[...]).astype(BF16), wuq_ref[...])
    kv = _dot(_rms(ckv, kvag_ref[...]).astype(BF16), wukv_ref[...])
    gq_n, gq_r = gq_ref[:, 0:LANES], gq_ref[:, LANES:2 * LANES]
    gk_n, gk_r = gk_ref[:, 0:LANES], gk_ref[:, LANES:2 * LANES]
    kpe_ss = jnp.sum(kpe * kpe, axis=-1, keepdims=True)
    for h in range(MLA_HEADS):
        lo = h * MLA_HEAD_PAD
        qn, qr = q[:, lo:lo + LANES], q[:, lo + LANES:lo + 2 * LANES]
        ms = (jnp.sum(qn * qn, axis=-1, keepdims=True)
              + jnp.sum(qr * qr, axis=-1, keepdims=True)) * (1.0 / dh)
        inv = lax.rsqrt(ms + EPS) * qscale
        qo_ref[:, lo:lo + LANES] = (qn * inv * gq_n).astype(qo_ref.dtype)
        qo_ref[:, lo + LANES:lo + 2 * LANES] = _rope_block(
            qr * inv * gq_r, ct, sat, sbt, half).astype(qo_ref.dtype)
        kn = kv[:, h * LANES:(h + 1) * LANES]
        ms = (jnp.sum(kn * kn, axis=-1, keepdims=True) + kpe_ss) * (1.0 / dh)
        inv = lax.rsqrt(ms + EPS)
        ko_ref[:, lo:lo + LANES] = (kn * inv * gk_n).astype(ko_ref.dtype)
        ko_ref[:, lo + LANES:lo + 2 * LANES] = _rope_block(
            kpe * inv * gk_r, ct, sat, sbt, half).astype(ko_ref.dtype)
    vo_ref[...] = kv[:, MLA_HEADS * LANES:].astype(vo_ref.dtype)


def _mla_prep(proj, qag, wuq, kvag, wukv, gq, gk, tabs, seq, tm, qscale):
    m = proj.shape[0]
    nrow = seq // tm
    hp = MLA_HEADS * MLA_HEAD_PAD
    tab_spec = pl.BlockSpec((tm, LANES), lambda i: (i % nrow, 0))
    const = lambda shape: pl.BlockSpec(shape, lambda i: (0, 0))
    return pl.pallas_call(
        functools.partial(_mla_prep_kernel, qscale=qscale),
        grid=(m // tm,),
        in_specs=[pl.BlockSpec((tm, C_WIDTH), lambda i: (i, OFF_C // C_WIDTH)),
                  const((1, MLA_Q_LORA)), const((MLA_Q_LORA, hp)),
                  const((1, MLA_KV_LORA)), const((MLA_KV_LORA, 2 * MLA_HEADS * LANES)),
                  const((1, MLA_HEAD_PAD)), const((1, MLA_HEAD_PAD)),
                  tab_spec, tab_spec, tab_spec],
        out_specs=[pl.BlockSpec((tm, hp), lambda i: (i, 0)),
                   pl.BlockSpec((tm, hp), lambda i: (i, 0)),
                   pl.BlockSpec((tm, MLA_HEADS * MLA_V), lambda i: (i, 0))],
        out_shape=[jax.ShapeDtypeStruct((m, hp), BF16),
                   jax.ShapeDtypeStruct((m, hp), BF16),
                   jax.ShapeDtypeStruct((m, MLA_HEADS * MLA_V), BF16)],
        compiler_params=_cparams(("parallel",)),
        name="mla_prep",
    )(proj, qag, wuq, kvag, wukv, gq, gk, *tabs)


def _attn_kernel(q_ref, k_ref, v_ref, lam_ref, sg_ref, o_ref, vext, m_scr, acc_scr,
                 *, seq, tk, diff, out_scale):
    tq = q_ref.shape[0]

    @pl.when(pl.program_id(2) == 0)
    def _():
        vext[:, 0:LANES] = v_ref[...]
        vext[:, LANES:2 * LANES] = jnp.ones((seq, LANES), BF16)

    q = q_ref[...]
    if diff:
        lane = lax.broadcasted_iota(jnp.int32, q.shape, 1)
        zero = jnp.zeros_like(q)
        q = jnp.concatenate([jnp.where(lane < DIFF_DH, q, zero),
                             jnp.where(lane >= DIFF_DH, q, zero)], axis=0)
    rows = q.shape[0]
    m_scr[...] = jnp.full((rows, LANES), -1e30, F32)
    acc_scr[...] = jnp.zeros((rows, 2 * LANES), F32)

    def body(j, carry):
        off = pl.multiple_of(j * tk, tk)
        s = _dot_nt(q, k_ref[pl.ds(off, tk), :])
        m_prev = m_scr[...]
        m_new = jnp.maximum(m_prev, jnp.max(s, axis=1, keepdims=True))
        alpha = jnp.exp2(m_prev - m_new)
        p = jnp.exp2(s - m_new[:, 0:1]).astype(BF16)
        acc_scr[...] = (acc_scr[...] * jnp.concatenate([alpha, alpha], axis=1)
                        + _dot(p, vext[pl.ds(off, tk), :]))
        m_scr[...] = m_new
        return carry

    lax.fori_loop(0, seq // tk, body, 0)

    acc = acc_scr[...]
    o = acc[:, 0:LANES] / acc[:, LANES:2 * LANES]
    if diff:
        o = o[0:tq, :] - lam_ref[...] * o[tq:2 * tq, :]
        o = _rms(o, sg_ref[...]) * out_scale
    o_ref[...] = o.astype(o_ref.dtype)


def _attention(q, k, v, v_off, lam, sg, batch, seq, heads, dq, tq, tk, diff, out_scale):
    nq = seq // tq
    rows = 2 * tq if diff else tq
    vblk = v_off // LANES
    return pl.pallas_call(
        functools.partial(_attn_kernel, seq=seq, tk=tk, diff=diff, out_scale=out_scale),
        grid=(batch, heads, nq),
        in_specs=[pl.BlockSpec((tq, dq), lambda b, h, i: (b * nq + i, h)),
                  pl.BlockSpec((seq, dq), lambda b, h, i: (b, h)),
                  pl.BlockSpec((seq, LANES), lambda b, h, i: (b, vblk + h)),
                  pl.BlockSpec((1, LANES), lambda b, h, i: (0, 0)),
                  pl.BlockSpec((1, LANES), lambda b, h, i: (0, 0))],
        out_specs=pl.BlockSpec((tq, LANES), lambda b, h, i: (b * nq + i, h)),
        out_shape=jax.ShapeDtypeStruct((batch * seq, heads * LANES), BF16),
        scratch_shapes=[pltpu.VMEM((seq, 2 * LANES), BF16),
                        pltpu.VMEM((rows, LANES), F32),
                        pltpu.VMEM((rows, 2 * LANES), F32)],
        compiler_params=_cparams(("parallel", "parallel", "arbitrary")),
        name="diff_attn" if diff else "mla_attn",
    )(q, k, v, lam, sg)


def _retention_kernel(q_ref, k_ref, v_ref, g_ref, lg_ref, gn_ref, c_tab, sa_tab, sb_tab,
                      o_ref, qs, ks, oacc, *, seq):
    ch = RET_CHUNK if seq % RET_CHUNK == 0 else seq
    nc = seq // ch
    half = RET_DK // 2
    ct, sat, sbt = c_tab[...], sa_tab[...], sb_tab[...]
    qs[...] = _rope_block(q_ref[...].astype(F32), ct, sat, sbt, half)
    ks[...] = _rope_block(k_ref[...].astype(F32), ct, sat, sbt, half) * (RET_DK ** -0.5)

    lane = lax.broadcasted_iota(jnp.int32, (ch, LANES), 1)
    ri = lax.broadcasted_iota(jnp.int32, (ch, ch), 0).astype(F32)
    ci = lax.broadcasted_iota(jnp.int32, (ch, ch), 1).astype(F32)
    idx = lax.broadcasted_iota(jnp.int32, (ch, 1), 0).astype(F32)

    for hh in range(2):
        lg = lg_ref[hh:hh + 1, 0:1]
        mask = (lane < RET_DK) if hh == 0 else (lane >= RET_DK)
        d_intra = jnp.exp(lg * jnp.abs(ri - ci))
        d_tail = jnp.exp(lg * (ch - 1.0 - idx))
        d_head = jnp.exp(lg * idx)
        d_qf = jnp.exp(lg * (idx + 1.0))
        d_qb = jnp.exp(lg * (ch - idx))
        d_chunk = jnp.exp(lg * float(ch))
        vlo = hh * RET_DV

        def load(n):
            off = pl.multiple_of(n * ch, ch)
            qc = jnp.where(mask, qs[pl.ds(off, ch), :], 0.0)
            kc = jnp.where(mask, ks[pl.ds(off, ch), :], 0.0)
            vc = v_ref[pl.ds(off, ch), vlo:vlo + RET_DV]
            return off, qc, kc, vc

        def fwd(n, past):
            off, qc, kc, vc = load(n)
            sc = _dot_nt(qc.astype(BF16), kc.astype(BF16)) * d_intra
            o = _dot(sc.astype(BF16), vc) + _dot((qc * d_qf).astype(BF16), past.astype(BF16))
            oacc[pl.ds(off, ch), :] = o
            return past * d_chunk + _dot_tn((kc * d_tail).astype(BF16), vc)

        lax.fori_loop(0, nc, fwd, jnp.zeros((LANES, RET_DV), F32))

        def bwd(t, fut):
            off, qc, kc, vc = load(nc - 1 - t)
            oacc[pl.ds(off, ch), :] += _dot((qc * d_qb).astype(BF16), fut.astype(BF16))
            return fut * d_chunk + _dot_tn((kc * d_head).astype(BF16), vc)

        lax.fori_loop(0, nc, bwd, jnp.zeros((LANES, RET_DV), F32))

        def fin(n, carry):
            off = pl.multiple_of(n * ch, ch)
            o = oacc[pl.ds(off, ch), :]
            oc = o - jnp.mean(o, axis=-1, keepdims=True)
            var = jnp.mean(oc * oc, axis=-1, keepdims=True)
            y = oc * lax.rsqrt(var + EPS) * gn_ref[...]
            g = g_ref[pl.ds(off, ch), vlo:vlo + RET_DV].astype(F32)
            o_ref[pl.ds(off, ch), vlo:vlo + RET_DV] = (g * _sigmoid(g) * y).astype(o_ref.dtype)
            return carry

        lax.fori_loop(0, nc, fin, 0)


def _retention(proj, lg, gn, tabs, batch, seq):
    npair = RET_HEADS // 2
    pw = 2 * RET_DV
    tab_spec = pl.BlockSpec((seq, LANES), lambda b, j: (0, 0))
    return pl.pallas_call(
        functools.partial(_retention_kernel, seq=seq),
        grid=(batch, npair),
        in_specs=[pl.BlockSpec((seq, LANES), lambda b, j: (b, OFF_DQ // LANES + j)),
                  pl.BlockSpec((seq, LANES), lambda b, j: (b, OFF_DK // LANES + j)),
                  pl.BlockSpec((seq, pw), lambda b, j: (b, OFF_DV // pw + j)),
                  pl.BlockSpec((seq, pw), lambda b, j: (b, OFF_DG // pw + j)),
                  pl.BlockSpec((None, 2, LANES), lambda b, j: (j, 0, 0)),
                  pl.BlockSpec((1, RET_DV), lambda b, j: (0, 0)),
                  tab_spec, tab_spec, tab_spec],
        out_specs=pl.BlockSpec((seq, pw), lambda b, j: (b, j)),
        out_shape=jax.ShapeDtypeStruct((batch * seq, RET_HEADS * RET_DV), BF16),
        scratch_shapes=[pltpu.VMEM((seq, LANES), F32),
                        pltpu.VMEM((seq, LANES), F32),
                        pltpu.VMEM((seq, RET_DV), F32)],
        compiler_params=_cparams(("parallel", "parallel")),
        name="retention",
    )(proj, proj, proj, proj, lg, gn, *tabs)


def _merge_kernel(x_ref, gl_ref, gb_ref, ya_ref, yb_ref, yc_ref, yd_ref,
                  wa_ref, wb_ref, wc_ref, wd_ref, wo_ref, o_ref):
    merged = None
    for n, (y_ref, w_ref) in enumerate(((ya_ref, wa_ref), (yb_ref, wb_ref),
                                        (yc_ref, wc_ref), (yd_ref, wd_ref))):
        lo = n * D_MODEL
        gate = _sigmoid(gl_ref[:, lo:lo + D_MODEL].astype(F32) + gb_ref[:, lo:lo + D_MODEL])
        term = gate * _dot(y_ref[...], w_ref[...])
        merged = term if merged is None else merged + term
    o_ref[...] = x_ref[...] + _dot(merged.astype(BF16), wo_ref[...])


def _merge(x2, proj, gb, ys, ws, wo, tm):
    m = x2.shape[0]
    row = lambda width: pl.BlockSpec((tm, width), lambda i: (i, 0))
    wspec = pl.BlockSpec((D_MODEL, D_MODEL), lambda i: (0, 0))
    return pl.pallas_call(
        _merge_kernel,
        grid=(m // tm,),
        in_specs=[row(D_MODEL), row(N_BRANCH * D_MODEL),
                  pl.BlockSpec((1, N_BRANCH * D_MODEL), lambda i: (0, 0)),
                  row(D_MODEL), row(D_MODEL), row(D_MODEL), row(D_MODEL),
                  wspec, wspec, wspec, wspec, wspec],
        out_specs=row(D_MODEL),
        out_shape=jax.ShapeDtypeStruct((m, D_MODEL), F32),
        compiler_params=_cparams(("parallel",)),
        name="merge",
    )(x2, proj, gb, *ys, *ws, wo)


def _ffn_kernel(x_ref, g2_ref, w1_ref, w2_ref, g3_ref, wpg_ref, p_ref, wpe_ref, o_ref,
                xn_ref, acc_ref):
    j = pl.program_id(1)

    @pl.when(j == 0)
    def _():
        xn_ref[...] = _rms(x_ref[...], g2_ref[...]).astype(BF16)
        acc_ref[...] = jnp.zeros_like(acc_ref)

    h = jnp.maximum(_dot(xn_ref[...], w1_ref[...]), 0.0)
    acc_ref[...] += _dot((h * h).astype(BF16), w2_ref[...])

    @pl.when(j == pl.num_programs(1) - 1)
    def _():
        x = x_ref[...] + acc_ref[...]
        gate = _sigmoid(_dot(_rms(x, g3_ref[...]).astype(BF16), wpg_ref[...]))
        o_ref[...] = x + gate * _dot(p_ref[...].astype(BF16), wpe_ref[...])


def _ffn(x2, g2, w1, w2, g3, wpg, p2, wpe, tm, tf):
    m = x2.shape[0]
    dff = w1.shape[1]
    pdim = p2.shape[1]
    return pl.pallas_call(
        _ffn_kernel,
        grid=(m // tm, dff // tf),
        in_specs=[pl.BlockSpec((tm, D_MODEL), lambda i, j: (i, 0)),
                  pl.BlockSpec((1, D_MODEL), lambda i, j: (0, 0)),
                  pl.BlockSpec((D_MODEL, tf), lambda i, j: (0, j)),
                  pl.BlockSpec((tf, D_MODEL), lambda i, j: (j, 0)),
                  pl.BlockSpec((1, D_MODEL), lambda i, j: (0, 0)),
                  pl.BlockSpec((D_MODEL, D_MODEL), lambda i, j: (0, 0)),
                  pl.BlockSpec((tm, pdim), lambda i, j: (i, 0)),
                  pl.BlockSpec((pdim, D_MODEL), lambda i, j: (0, 0))],
        out_specs=pl.BlockSpec((tm, D_MODEL), lambda i, j: (i, 0)),
        out_shape=jax.ShapeDtypeStruct((m, D_MODEL), F32),
        scratch_shapes=[pltpu.VMEM((tm, D_MODEL), BF16), pltpu.VMEM((tm, D_MODEL), F32)],
        compiler_params=_cparams(("parallel", "arbitrary")),
        name="ffn_ple",
    )(x2, g2, w1, w2, g3, wpg, p2, wpe)


def _block_diag(blocks):
    n, c, d = blocks.shape
    eye = jnp.eye(n, dtype=blocks.dtype)
    return jnp.einsum('ncd,nm->ncmd', blocks, eye).reshape(n * c, n * d)


def _permute_w_in(w_in):
    sizes = (1024, 1024, 1024, 1024, 1024, 384, 256, 64, 512, 512, 1024, 1024, 4096)
    offs = [0]
    for s in sizes:
        offs.append(offs[-1] + s)
    (a_x, a_g, b_q, b_k, b_v, c_q, c_kv, c_kpe, d_q, d_k, d_v, d_g, gates) = [
        w_in[:, offs[n]:offs[n + 1]] for n in range(len(sizes))]
    pad = jnp.zeros((w_in.shape[0], C_WIDTH - 704), w_in.dtype)
    return jnp.concatenate([gates, a_x, a_g, b_q, b_k, b_v, d_v, d_g, d_q, d_k,
                            c_q, c_kv, c_kpe, pad], axis=1).astype(BF16)


def _lru_weights(wa, ba, wx, bx):
    w = MXU_DIM
    nt = D_MODEL // w
    dense = [_block_diag(m) for m in (wa[0], wx[0], wa[1], wx[1])]
    wd = jnp.stack([jnp.concatenate([d[c * w:(c + 1) * w, c * w:(c + 1) * w] for d in dense], axis=1)
                    for c in range(nt)]).astype(BF16)
    bias = (ba[0], bx[0], ba[1], bx[1])
    bd = jnp.stack([jnp.concatenate([b[c * w:(c + 1) * w] for b in bias])[None, :]
                    for c in range(nt)])
    return wd, bd


def _mla_weights(wuq, wukv, gq, gk):
    dh = MLA_NOPE + MLA_ROPE
    zpad = MLA_HEAD_PAD - dh
    wq = wuq.reshape(MLA_Q_LORA, MLA_HEADS, dh)
    wq = jnp.pad(wq, ((0, 0), (0, 0), (0, zpad))).reshape(MLA_Q_LORA, MLA_HEADS * MLA_HEAD_PAD)
    wkv = wukv.reshape(MLA_KV_LORA, MLA_HEADS, MLA_NOPE + MLA_V)
    wk = wkv[:, :, :MLA_NOPE].reshape(MLA_KV_LORA, MLA_HEADS * MLA_NOPE)
    wv = wkv[:, :, MLA_NOPE:].reshape(MLA_KV_LORA, MLA_HEADS * MLA_V)
    gqp = jnp.pad(gq, (0, zpad))[None, :]
    gkp = jnp.pad(gk, (0, zpad))[None, :]
    return wq.astype(BF16), jnp.concatenate([wk, wv], axis=1).astype(BF16), gqp, gkp


def kernel(x, p, norm1_g, w_in, gate_b, conv_w, conv_b, lru_wa, lru_ba, lru_wx, lru_bx, lru_lambda,
           diff_q_g, diff_k_g, diff_lam, diff_sub_g, mla_qa_g, mla_wuq, mla_kva_g, mla_wukv,
           mla_q_g, mla_k_g, ret_gn_g, w_br_a, w_br_b, w_br_c, w_br_d, w_out, norm2_g,
           w_ff1, w_ff2, norm3_g, w_ple_gate, w_ple_proj):
    batch, seq, d_model = x.shape
    depth = w_in.shape[0]
    assert d_model == D_MODEL and seq % (8 * N_SEG) == 0
    m = batch * seq
    tm_big = min(1024, seq)
    tm_small = min(512, seq)
    tq = min(256, seq)
    tk = min(512, seq)

    tabs_b = _rope_lane_tables(seq, DIFF_ROT, DIFF_DH, ROPE_THETA)
    tabs_c = _rope_lane_tables(seq, MLA_ROPE, LANES, ROPE_THETA)
    tabs_d = _rope_lane_tables(seq, RET_DK, RET_DK, RET_THETA)
    log_g = jnp.log1p(-jnp.exp2(-5.0 - jnp.arange(RET_HEADS, dtype=F32)))
    lg = jnp.broadcast_to(log_g.reshape(RET_HEADS // 2, 2, 1), (RET_HEADS // 2, 2, LANES))
    grp = jnp.arange(MXU_DIM) // DIFF_DH
    gm = ((grp[:, None] == grp[None, :]).astype(F32) / DIFF_DH).astype(BF16)
    ones_row = jnp.ones((1, LANES), F32)

    x2 = x.reshape(m, d_model)
    for i in range(depth):
        proj = _inproj(x2, norm1_g[i][None, :], _permute_w_in(w_in[i]), tm_big, C_WIDTH)

        wd, bd = _lru_weights(lru_wa[i], lru_ba[i], lru_wx[i], lru_bx[i])
        y_a = _rglru(proj, conv_w[i], conv_b[i][None, :], wd, bd, lru_lambda[i], batch, seq)

        lam_init = 0.8 - 0.6 * math.exp(-0.3 * i)
        lp = diff_lam[i].astype(F32)
        lam = jnp.exp(jnp.sum(lp[0] * lp[1])) - jnp.exp(jnp.sum(lp[2] * lp[3])) + lam_init
        gq = jnp.tile(diff_q_g[i], MXU_DIM // DIFF_DH)[None, :]
        gk = jnp.tile(diff_k_g[i], MXU_DIM // DIFF_DH)[None, :]
        qb, kb = _diff_prep(proj, gq, gk, tabs_b, gm, seq, tm_small, DIFF_DH ** -0.5 * LOG2E)
        y_b = _attention(qb, kb, proj, OFF_BV, jnp.full((1, LANES), lam, F32), diff_sub_g[i][None, :],
                         batch, seq, DIFF_HEADS, LANES, tq, tk, True, 1.0 - lam_init)

        wq, wkv, gqp, gkp = _mla_weights(mla_wuq[i], mla_wukv[i], mla_q_g[i], mla_k_g[i])
        qc, kc, vc = _mla_prep(proj, mla_qa_g[i][None, :], wq, mla_kva_g[i][None, :], wkv,
                               gqp, gkp, tabs_c, seq, tm_small,
                               (MLA_NOPE + MLA_ROPE) ** -0.5 * LOG2E)
        y_c = _attention(qc, kc, vc, 0, ones_row, ones_row, batch, seq, MLA_HEADS, MLA_HEAD_PAD,
                         tq, tk, False, 1.0)

        y_d = _retention(proj, lg, ret_gn_g[i][None, :], tabs_d, batch, seq)

        x2 = _merge(x2, proj, gate_b[i].reshape(1, N_BRANCH * D_MODEL), (y_a, y_b, y_c, y_d),
                    tuple(w[i].astype(BF16) for w in (w_br_a, w_br_b, w_br_c, w_br_d)),
                    w_out[i].astype(BF16), min(256, seq))

        x2 = _ffn(x2, norm2_g[i][None, :], w_ff1[i].astype(BF16), w_ff2[i].astype(BF16),
                  norm3_g[i][None, :], w_ple_gate[i].astype(BF16),
                  p[i].reshape(m, p.shape[-1]), w_ple_proj[i].astype(BF16), tm_big, 512)
    return x2.reshape(batch, seq, d_model)
```

```python
import functools
import math

import jax
import jax.numpy as jnp
from jax import lax
from jax.experimental import pallas as pl
from jax.experimental.pallas import tpu as pltpu

F32 = jnp.float32
BF16 = jnp.bfloat16

EPS = 1e-6
ROPE_THETA = 500000.0
RET_THETA = 10000.0
LRU_C = 8.0
LOG2E = 1.4426950408889634

LANES = 128
MXU_DIM = 256
VMEM_LIMIT = 56 * 1024 * 1024

D_MODEL = 1024
N_BRANCH = 4
LRU_BLOCKS = 16
CONV_WIDTH = 4
DIFF_HEADS = 8
DIFF_DH = 64
DIFF_ROT = 16
MLA_HEADS = 8
MLA_NOPE = 128
MLA_ROPE = 64
MLA_V = 128
MLA_Q_LORA = 384
MLA_KV_LORA = 256
MLA_HEAD_PAD = 256
RET_HEADS = 8
RET_DK = 64
RET_DV = 128
RET_CHUNK = 256
ATTN_ROW_BLOCK = 128
MAX_SCORE_LOG2 = 60.0
N_SEG = 8
SEG_PAD = 4

OFF_GATE = 0
OFF_AX = 4096
OFF_AG = 5120
OFF_BQ = 6144
OFF_BK = 7168
OFF_BV = 8192
OFF_DV = 9216
OFF_DG = 10240
OFF_DQ = 11264
OFF_DK = 11776
OFF_C = 12288
C_WIDTH = 768
N_PROJ = OFF_C + C_WIDTH


def _cparams(sem):
    return pltpu.CompilerParams(dimension_semantics=sem, vmem_limit_bytes=VMEM_LIMIT)


def _sigmoid(x):
    return 1.0 / (1.0 + jnp.exp(-x))


def _rms(x, g):
    return x * lax.rsqrt(jnp.mean(x * x, axis=-1, keepdims=True) + EPS) * g


def _dot(a, b):
    return jnp.dot(a, b, preferred_element_type=F32)


def _dot_nt(a, b):
    return lax.dot_general(a, b, (((1,), (1,)), ((), ())), preferred_element_type=F32)


def _dot_tn(a, b):
    return lax.dot_general(a, b, (((0,), (0,)), ((), ())), preferred_element_type=F32)


def _inproj_kernel(x_ref, g_ref, w_ref, o_ref, xn_ref):
    @pl.when(pl.program_id(1) == 0)
    def _():
        xn_ref[...] = _rms(x_ref[...], g_ref[...]).astype(BF16)

    o_ref[...] = _dot(xn_ref[...], w_ref[...]).astype(o_ref.dtype)


def _inproj(x2, g, w, tm, tn):
    m, k = x2.shape
    n = w.shape[1]
    return pl.pallas_call(
        _inproj_kernel,
        grid=(m // tm, n // tn),
        in_specs=[pl.BlockSpec((tm, k), lambda i, j: (i, 0)),
                  pl.BlockSpec((1, k), lambda i, j: (0, 0)),
                  pl.BlockSpec((k, tn), lambda i, j: (0, j))],
        out_specs=pl.BlockSpec((tm, tn), lambda i, j: (i, j)),
        out_shape=jax.ShapeDtypeStruct((m, n), BF16),
        scratch_shapes=[pltpu.VMEM((tm, k), BF16)],
        compiler_params=_cparams(("parallel", "arbitrary")),
        name="inproj",
    )(x2, g, w)


def _gelu_tanh(x):
    return 0.5 * x * (1.0 + jnp.tanh(0.7978845608028654 * (x + 0.044715 * x * x * x)))


def _rglru_kernel(x_ref, ga_ref, cw_ref, cb_ref, wd_ref, bd_ref, lam_ref, o_ref,
                  xpad, a0, u0, a1, u1, *, seq):
    w = MXU_DIM
    nslab = w // LANES
    seg = seq // N_SEG
    pitch = seg + SEG_PAD
    left = CONV_WIDTH // 2

    xpad[0:8, :] = jnp.zeros((8, w), F32)
    xpad[seq + 8:seq + 16, :] = jnp.zeros((8, w), F32)
    for s in range(N_SEG):
        xpad[8 + s * seg:8 + (s + 1) * seg, :] = x_ref[s * seg:(s + 1) * seg, :].astype(F32)

    lam = lam_ref[...]
    nlam = -lam
    nls = -LRU_C * (jnp.maximum(nlam, 0.0) + jnp.log(1.0 + jnp.exp(-jnp.abs(nlam))))

    a_scr = (a0, a1)
    u_scr = (u0, u1)
    for s in range(N_SEG):
        base = 8 + s * seg - left
        xc = cb_ref[...]
        for t in range(CONV_WIDTH):
            xc = xc + xpad[base + t:base + t + seg, :] * cw_ref[t:t + 1, :]
        z = _dot(xc.astype(BF16), wd_ref[0]) + bd_ref[0]
        for d in range(2):
            r = _sigmoid(z[:, (2 * d) * w:(2 * d + 1) * w])
            gi = _sigmoid(z[:, (2 * d + 1) * w:(2 * d + 2) * w])
            log_a = nls[d:d + 1, :] * r
            a = jnp.exp(log_a)
            u = jnp.sqrt(1.0 - jnp.exp(2.0 * log_a)) * (gi * xc)
            for sl in range(nslab):
                a_scr[d][sl, s * pitch:s * pitch + seg, :] = a[:, sl * LANES:(sl + 1) * LANES]
                u_scr[d][sl, s * pitch:s * pitch + seg, :] = u[:, sl * LANES:(sl + 1) * LANES]

    def rows(t):
        return pl.ds(t, N_SEG, stride=pitch)

    def scan_body(t, carry):
        new = []
        tb = seg - 1 - t
        for sl in range(nslab):
            hf, pf, hb, pb = carry[sl]
            af = a0[sl, rows(t), :]
            hf = af * hf + u0[sl, rows(t), :]
            pf = pf * af
            u0[sl, rows(t), :] = hf
            a0[sl, rows(t), :] = pf
            ab = a1[sl, rows(tb), :]
            hb = ab * hb + u1[sl, rows(tb), :]
            pb = pb * ab
            u1[sl, rows(tb), :] = hb
            a1[sl, rows(tb), :] = pb
            new.append((hf, pf, hb, pb))
        return tuple(new)

    zero = jnp.zeros((N_SEG, LANES), F32)
    one = jnp.ones((N_SEG, LANES), F32)
    fin = lax.fori_loop(0, seg, scan_body, tuple((zero, one, zero, one) for _ in range(nslab)))

    cf, cb = [], []
    for sl in range(nslab):
        hf, pf, hb, pb = fin[sl]
        c = jnp.zeros((1, LANES), F32)
        cfr = [c]
        for s in range(1, N_SEG):
            c = pf[s - 1:s, :] * c + hf[s - 1:s, :]
            cfr.append(c)
        cf.append(jnp.concatenate(cfr, axis=0))
        c = jnp.zeros((1, LANES), F32)
        cbr = [c]
        for s in range(N_SEG - 2, -1, -1):
            c = pb[s + 1:s + 2, :] * c + hb[s + 1:s + 2, :]
            cbr.append(c)
        cb.append(jnp.concatenate(cbr[::-1], axis=0))

    def fix_body(t, carry):
        for sl in range(nslab):
            hf = u0[sl, rows(t), :] + a0[sl, rows(t), :] * cf[sl]
            hb = u1[sl, rows(t), :] + a1[sl, rows(t), :] * cb[sl]
            u0[sl, rows(t), :] = hf + hb
        return carry

    lax.fori_loop(0, seg, fix_body, 0)

    for s in range(N_SEG):
        g = _gelu_tanh(ga_ref[s * seg:(s + 1) * seg, :].astype(F32))
        for sl in range(nslab):
            h = u0[sl, s * pitch:s * pitch + seg, :]
            o_ref[s * seg:(s + 1) * seg, sl * LANES:(sl + 1) * LANES] = (
                h * g[:, sl * LANES:(sl + 1) * LANES]).astype(o_ref.dtype)


def _rglru(proj, cw, cb, wd, bd, lam, batch, seq):
    w = MXU_DIM
    nt = D_MODEL // w
    nslab = w // LANES
    rows = N_SEG * (seq // N_SEG + SEG_PAD)
    scan_scratch = pltpu.VMEM((nslab, rows, LANES), F32)
    return pl.pallas_call(
        functools.partial(_rglru_kernel, seq=seq),
        grid=(batch, nt),
        in_specs=[pl.BlockSpec((seq, w), lambda b, c: (b, OFF_AX // w + c)),
                  pl.BlockSpec((seq, w), lambda b, c: (b, OFF_AG // w + c)),
                  pl.BlockSpec((CONV_WIDTH, w), lambda b, c: (0, c)),
                  pl.BlockSpec((1, w), lambda b, c: (0, c)),
                  pl.BlockSpec((1, w, 4 * w), lambda b, c: (c, 0, 0)),
                  pl.BlockSpec((1, 1, 4 * w), lambda b, c: (c, 0, 0)),
                  pl.BlockSpec((2, w), lambda b, c: (0, c))],
        out_specs=pl.BlockSpec((seq, w), lambda b, c: (b, c)),
        out_shape=jax.ShapeDtypeStruct((batch * seq, D_MODEL), BF16),
        scratch_shapes=[pltpu.VMEM((seq + 16, w), F32),
                        scan_scratch, scan_scratch, scan_scratch, scan_scratch],
        compiler_params=_cparams(("parallel", "parallel")),
        name="rglru",
    )(proj, proj, cw, cb, wd, bd, lam)


def _rope_block(x, c, sa, sb, half):
    return (x * c + pltpu.roll(x, LANES - half, axis=1) * sa
            + pltpu.roll(x, half, axis=1) * sb)


def _rope_lane_tables(seq, rot, period, theta):
    half = rot // 2
    pos = jnp.arange(seq, dtype=F32)
    inv_freq = theta ** (-jnp.arange(0, rot, 2, dtype=F32) / rot)
    ang = pos[:, None] * inv_freq[None, :]
    cos, sin = jnp.cos(ang), jnp.sin(ang)
    pad = period - rot
    c1 = jnp.concatenate([cos, cos, jnp.ones((seq, pad), F32)], axis=1)
    s1 = jnp.concatenate([-sin, jnp.zeros((seq, half + pad), F32)], axis=1)
    s2 = jnp.concatenate([jnp.zeros((seq, half), F32), sin, jnp.zeros((seq, pad), F32)], axis=1)
    rep = LANES // period
    return jnp.tile(c1, (1, rep)), jnp.tile(s1, (1, rep)), jnp.tile(s2, (1, rep))


def _diff_prep_kernel(q_ref, k_ref, gq_ref, gk_ref, c_ref, sa_ref, sb_ref, gm_ref,
                      qo_ref, ko_ref, *, qscale):
    c, sa, sb = c_ref[...], sa_ref[...], sb_ref[...]
    gm = gm_ref[...]
    for src, g_ref, is_key in ((q_ref, gq_ref, False), (k_ref, gk_ref, True)):
        for blk in range(D_MODEL // MXU_DIM):
            x = src[:, blk * MXU_DIM:(blk + 1) * MXU_DIM].astype(F32)
            ms = _dot((x * x).astype(BF16), gm)
            xn = x * lax.rsqrt(ms + EPS) * g_ref[...]
            for h in range(MXU_DIM // LANES):
                xb = _rope_block(xn[:, h * LANES:(h + 1) * LANES], c, sa, sb, DIFF_ROT // 2)
                lo = blk * MXU_DIM + h * LANES
                if is_key:
                    ko_ref[lo:lo + LANES, :] = xb.T.astype(ko_ref.dtype)
                else:
                    qo_ref[:, lo:lo + LANES] = (xb * qscale).astype(qo_ref.dtype)


def _diff_prep(proj, gq, gk, tabs, gm, seq, tm, qscale):
    m = proj.shape[0]
    nrow = seq // tm
    tab_spec = pl.BlockSpec((tm, LANES), lambda i: (i % nrow, 0))
    return pl.pallas_call(
        functools.partial(_diff_prep_kernel, qscale=qscale),
        grid=(m // tm,),
        in_specs=[pl.BlockSpec((tm, D_MODEL), lambda i: (i, OFF_BQ // D_MODEL)),
                  pl.BlockSpec((tm, D_MODEL), lambda i: (i, OFF_BK // D_MODEL)),
                  pl.BlockSpec((1, MXU_DIM), lambda i: (0, 0)),
                  pl.BlockSpec((1, MXU_DIM), lambda i: (0, 0)),
                  tab_spec, tab_spec, tab_spec,
                  pl.BlockSpec((MXU_DIM, MXU_DIM), lambda i: (0, 0))],
        out_specs=[pl.BlockSpec((tm, D_MODEL), lambda i: (i, 0)),
                   pl.BlockSpec((None, D_MODEL, tm), lambda i: (i // nrow, 0, i % nrow))],
        out_shape=[jax.ShapeDtypeStruct((m, D_MODEL), BF16),
                   jax.ShapeDtypeStruct((m // seq, D_MODEL, seq), BF16)],
        compiler_params=_cparams(("parallel",)),
        name="diff_prep",
    )(proj, proj, gq, gk, *tabs, gm)


def _mla_prep_kernel(c_ref, qag_ref, wuq_ref, kvag_ref, wukv_ref, gq_ref, gk_ref,
                     c_tab, sa_tab, sb_tab, qo_ref, ko_ref, vo_ref, *, qscale):
    dh = MLA_NOPE + MLA_ROPE
    half = MLA_ROPE // 2
    ct, sat, sbt = c_tab[...], sa_tab[...], sb_tab[...]
    cq = c_ref[:, 0:MLA_Q_LORA].astype(F32)
    ckv = c_ref[:, MLA_Q_LORA:MLA_Q_LORA + MLA_KV_LORA].astype(F32)
    kpe = c_ref[:, MLA_Q_LORA + MLA_KV_LORA:C_WIDTH].astype(F32)
    q = _dot(_rms(cq, qag_ref[...]).astype(BF16), wuq_ref[...])
    kv = _dot(_rms(ckv, kvag_ref[...]).astype(BF16), wukv_ref[...])
    gq_n, gq_r = gq_ref[:, 0:LANES], gq_ref[:, LANES:2 * LANES]
    gk_n, gk_r = gk_ref[:, 0:LANES], gk_ref[:, LANES:2 * LANES]
    kpe_ss = jnp.sum(kpe * kpe, axis=-1, keepdims=True)
    for h in range(MLA_HEADS):
        lo = h * MLA_HEAD_PAD
        qn, qr = q[:, lo:lo + LANES], q[:, lo + LANES:lo + 2 * LANES]
        ms = (jnp.sum(qn * qn, axis=-1, keepdims=True)
              + jnp.sum(qr * qr, axis=-1, keepdims=True)) * (1.0 / dh)
        inv = lax.rsqrt(ms + EPS) * qscale
        qo_ref[:, lo:lo + LANES] = (qn * inv * gq_n).astype(qo_ref.dtype)
        qo_ref[:, lo + LANES:lo + 2 * LANES] = _rope_block(
            qr * inv * gq_r, ct, sat, sbt, half).astype(qo_ref.dtype)
        kn = kv[:, h * LANES:(h + 1) * LANES]
        ms = (jnp.sum(kn * kn, axis=-1, keepdims=True) + kpe_ss) * (1.0 / dh)
        inv = lax.rsqrt(ms + EPS)
        ko_ref[lo:lo + LANES, :] = (kn * inv * gk_n).T.astype(ko_ref.dtype)
        ko_ref[lo + LANES:lo + 2 * LANES, :] = _rope_block(
            kpe * inv * gk_r, ct, sat, sbt, half).T.astype(ko_ref.dtype)
    vo_ref[...] = kv[:, MLA_HEADS * LANES:].astype(vo_ref.dtype)


def _mla_prep(proj, qag, wuq, kvag, wukv, gq, gk, tabs, seq, tm, qscale):
    m = proj.shape[0]
    nrow = seq // tm
    hp = MLA_HEADS * MLA_HEAD_PAD
    tab_spec = pl.BlockSpec((tm, LANES), lambda i: (i % nrow, 0))
    const = lambda shape: pl.BlockSpec(shape, lambda i: (0, 0))
    return pl.pallas_call(
        functools.partial(_mla_prep_kernel, qscale=qscale),
        grid=(m // tm,),
        in_specs=[pl.BlockSpec((tm, C_WIDTH), lambda i: (i, OFF_C // C_WIDTH)),
                  const((1, MLA_Q_LORA)), const((MLA_Q_LORA, hp)),
                  const((1, MLA_KV_LORA)), const((MLA_KV_LORA, 2 * MLA_HEADS * LANES)),
                  const((1, MLA_HEAD_PAD)), const((1, MLA_HEAD_PAD)),
                  tab_spec, tab_spec, tab_spec],
        out_specs=[pl.BlockSpec((tm, hp), lambda i: (i, 0)),
                   pl.BlockSpec((None, hp, tm), lambda i: (i // nrow, 0, i % nrow)),
                   pl.BlockSpec((tm, MLA_HEADS * MLA_V), lambda i: (i, 0))],
        out_shape=[jax.ShapeDtypeStruct((m, hp), BF16),
                   jax.ShapeDtypeStruct((m // seq, hp, seq), BF16),
                   jax.ShapeDtypeStruct((m, MLA_HEADS * MLA_V), BF16)],
        compiler_params=_cparams(("parallel",)),
        name="mla_prep",
    )(proj, qag, wuq, kvag, wukv, gq, gk, *tabs)


def _attn_kernel(q_ref, k_ref, v_ref, lam_ref, sg_ref, o_ref, vext, m_scr, acc_scr,
                 *, seq, tk, diff, out_scale):
    tq = q_ref.shape[0]

    @pl.when(pl.program_id(2) == 0)
    def _():
        vext[:, 0:LANES] = v_ref[...]
        vext[:, LANES:2 * LANES] = jnp.ones((seq, LANES), BF16)

    q = q_ref[...]
    if diff:
        lane = lax.broadcasted_iota(jnp.int32, q.shape, 1)
        zero = jnp.zeros_like(q)
        q = jnp.concatenate([jnp.where(lane < DIFF_DH, q, zero),
                             jnp.where(lane >= DIFF_DH, q, zero)], axis=0)
    rows = q.shape[0]
    m_scr[...] = jnp.full((rows, LANES), -1e30, F32)
    acc_scr[...] = jnp.zeros((rows, 2 * LANES), F32)

    def body(j, carry):
        off = pl.multiple_of(j * tk, tk)
        s = _dot(q, k_ref[:, pl.ds(off, tk)])
        m_prev = m_scr[...]
        m_new = jnp.maximum(m_prev, jnp.max(s, axis=1, keepdims=True))
        alpha = jnp.exp2(m_prev - m_new)
        p = jnp.exp2(s - m_new[:, 0:1]).astype(BF16)
        acc_scr[...] = (acc_scr[...] * jnp.concatenate([alpha, alpha], axis=1)
                        + _dot(p, vext[pl.ds(off, tk), :]))
        m_scr[...] = m_new
        return carry

    lax.fori_loop(0, seq // tk, body, 0)

    acc = acc_scr[...]
    o = acc[:, 0:LANES] / acc[:, LANES:2 * LANES]
    if diff:
        o = o[0:tq, :] - lam_ref[...] * o[tq:2 * tq, :]
        o = _rms(o, sg_ref[...]) * out_scale
    o_ref[...] = o.astype(o_ref.dtype)


def _attn_bounded_kernel(q_ref, k_ref, v_ref, lam_ref, sg_ref, o_ref, vext, m_scr, acc_scr,
                         *, seq, tk, diff, out_scale):
    del m_scr
    tq = q_ref.shape[0]

    @pl.when(pl.program_id(2) == 0)
    def _():
        vext[:, 0:LANES] = v_ref[...]
        vext[:, LANES:2 * LANES] = jnp.ones((seq, LANES), BF16)

    q = q_ref[...]
    if diff:
        lane = lax.broadcasted_iota(jnp.int32, q.shape, 1)
        zero = jnp.zeros_like(q)
        q = jnp.concatenate([jnp.where(lane < DIFF_DH, q, zero),
                             jnp.where(lane >= DIFF_DH, q, zero)], axis=0)
    rows = q.shape[0]
    rb = ATTN_ROW_BLOCK
    acc_scr[...] = jnp.zeros((rows, 2 * LANES), F32)

    def body(j, carry):
        off = pl.multiple_of(j * tk, tk)
        kt = k_ref[:, pl.ds(off, tk)]
        vt = vext[pl.ds(off, tk), :]
        for r in range(rows // rb):
            p = jnp.exp2(_dot(q[r * rb:(r + 1) * rb, :], kt)).astype(BF16)
            acc_scr[r * rb:(r + 1) * rb, :] += _dot(p, vt)
        return carry

    lax.fori_loop(0, seq // tk, body, 0, unroll=True)

    acc = acc_scr[...]
    o = acc[:, 0:LANES] / acc[:, LANES:2 * LANES]
    if diff:
        o = o[0:tq, :] - lam_ref[...] * o[tq:2 * tq, :]
        o = _rms(o, sg_ref[...]) * out_scale
    o_ref[...] = o.astype(o_ref.dtype)


def _score_bound(gq, gk, dh):
    return 1.02 * LOG2E * math.sqrt(dh) * jnp.max(jnp.abs(gq)) * jnp.max(jnp.abs(gk))


def _attention(q, k, v, lam, sg, *, v_off, batch, seq, heads, dq, tq, tk, diff, out_scale, bounded):
    nq = seq // tq
    rows = 2 * tq if diff else tq
    vblk = v_off // LANES
    body = _attn_bounded_kernel if bounded else _attn_kernel
    return pl.pallas_call(
        functools.partial(body, seq=seq, tk=tk, diff=diff, out_scale=out_scale),
        grid=(batch, heads, nq),
        in_specs=[pl.BlockSpec((tq, dq), lambda b, h, i: (b * nq + i, h)),
                  pl.BlockSpec((None, dq, seq), lambda b, h, i: (b, h, 0)),
                  pl.BlockSpec((seq, LANES), lambda b, h, i: (b, vblk + h)),
                  pl.BlockSpec((1, LANES), lambda b, h, i: (0, 0)),
                  pl.BlockSpec((1, LANES), lambda b, h, i: (0, 0))],
        out_specs=pl.BlockSpec((tq, LANES), lambda b, h, i: (b * nq + i, h)),
        out_shape=jax.ShapeDtypeStruct((batch * seq, heads * LANES), BF16),
        scratch_shapes=[pltpu.VMEM((seq, 2 * LANES), BF16),
                        pltpu.VMEM((rows, LANES), F32),
                        pltpu.VMEM((rows, 2 * LANES), F32)],
        compiler_params=_cparams(("parallel", "parallel", "arbitrary")),
        name=("diff_attn" if diff else "mla_attn") + ("_bounded" if bounded else "_online"),
    )(q, k, v, lam, sg)


def _retention_kernel(q_ref, k_ref, v_ref, g_ref, lg_ref, gn_ref, c_tab, sa_tab, sb_tab,
                      o_ref, qs, ks, oacc, *, seq):
    ch = RET_CHUNK if seq % RET_CHUNK == 0 else seq
    nc = seq // ch
    half = RET_DK // 2
    ct, sat, sbt = c_tab[...], sa_tab[...], sb_tab[...]
    qs[...] = _rope_block(q_ref[...].astype(F32), ct, sat, sbt, half)
    ks[...] = _rope_block(k_ref[...].astype(F32), ct, sat, sbt, half) * (RET_DK ** -0.5)

    lane = lax.broadcasted_iota(jnp.int32, (ch, LANES), 1)
    ri = lax.broadcasted_iota(jnp.int32, (ch, ch), 0).astype(F32)
    ci = lax.broadcasted_iota(jnp.int32, (ch, ch), 1).astype(F32)
    idx = lax.broadcasted_iota(jnp.int32, (ch, 1), 0).astype(F32)

    for hh in range(2):
        lg = lg_ref[hh:hh + 1, 0:1]
        mask = (lane < RET_DK) if hh == 0 else (lane >= RET_DK)
        d_intra = jnp.exp(lg * jnp.abs(ri - ci))
        d_tail = jnp.exp(lg * (ch - 1.0 - idx))
        d_head = jnp.exp(lg * idx)
        d_qf = jnp.exp(lg * (idx + 1.0))
        d_qb = jnp.exp(lg * (ch - idx))
        d_chunk = jnp.exp(lg * float(ch))
        vlo = hh * RET_DV

        def load(n):
            off = pl.multiple_of(n * ch, ch)
            qc = jnp.where(mask, qs[pl.ds(off, ch), :], 0.0)
            kc = jnp.where(mask, ks[pl.ds(off, ch), :], 0.0)
            vc = v_ref[pl.ds(off, ch), vlo:vlo + RET_DV]
            return off, qc, kc, vc

        def fwd(n, past):
            off, qc, kc, vc = load(n)
            sc = _dot_nt(qc.astype(BF16), kc.astype(BF16)) * d_intra
            o = _dot(sc.astype(BF16), vc) + _dot((qc * d_qf).astype(BF16), past.astype(BF16))
            oacc[pl.ds(off, ch), :] = o
            return past * d_chunk + _dot_tn((kc * d_tail).astype(BF16), vc)

        lax.fori_loop(0, nc, fwd, jnp.zeros((LANES, RET_DV), F32))

        def bwd(t, fut):
            off, qc, kc, vc = load(nc - 1 - t)
            oacc[pl.ds(off, ch), :] += _dot((qc * d_qb).astype(BF16), fut.astype(BF16))
            return fut * d_chunk + _dot_tn((kc * d_head).astype(BF16), vc)

        lax.fori_loop(0, nc, bwd, jnp.zeros((LANES, RET_DV), F32))

        def fin(n, carry):
            off = pl.multiple_of(n * ch, ch)
            o = oacc[pl.ds(off, ch), :]
            oc = o - jnp.mean(o, axis=-1, keepdims=True)
            var = jnp.mean(oc * oc, axis=-1, keepdims=True)
            y = oc * lax.rsqrt(var + EPS) * gn_ref[...]
            g = g_ref[pl.ds(off, ch), vlo:vlo + RET_DV].astype(F32)
            o_ref[pl.ds(off, ch), vlo:vlo + RET_DV] = (g * _sigmoid(g) * y).astype(o_ref.dtype)
            return carry

        lax.fori_loop(0, nc, fin, 0)


def _retention(proj, lg, gn, tabs, batch, seq):
    npair = RET_HEADS // 2
    pw = 2 * RET_DV
    tab_spec = pl.BlockSpec((seq, LANES), lambda b, j: (0, 0))
    return pl.pallas_call(
        functools.partial(_retention_kernel, seq=seq),
        grid=(batch, npair),
        in_specs=[pl.BlockSpec((seq, LANES), lambda b, j: (b, OFF_DQ // LANES + j)),
                  pl.BlockSpec((seq, LANES), lambda b, j: (b, OFF_DK // LANES + j)),
                  pl.BlockSpec((seq, pw), lambda b, j: (b, OFF_DV // pw + j)),
                  pl.BlockSpec((seq, pw), lambda b, j: (b, OFF_DG // pw + j)),
                  pl.BlockSpec((None, 2, LANES), lambda b, j: (j, 0, 0)),
                  pl.BlockSpec((1, RET_DV), lambda b, j: (0, 0)),
                  tab_spec, tab_spec, tab_spec],
        out_specs=pl.BlockSpec((seq, pw), lambda b, j: (b, j)),
        out_shape=jax.ShapeDtypeStruct((batch * seq, RET_HEADS * RET_DV), BF16),
        scratch_shapes=[pltpu.VMEM((seq, LANES), F32),
                        pltpu.VMEM((seq, LANES), F32),
                        pltpu.VMEM((seq, RET_DV), F32)],
        compiler_params=_cparams(("parallel", "parallel")),
        name="retention",
    )(proj, proj, proj, proj, lg, gn, *tabs)


def _merge_kernel(x_ref, gl_ref, gb_ref, ya_ref, yb_ref, yc_ref, yd_ref,
                  wa_ref, wb_ref, wc_ref, wd_ref, wo_ref, o_ref):
    merged = None
    for n, (y_ref, w_ref) in enumerate(((ya_ref, wa_ref), (yb_ref, wb_ref),
                                        (yc_ref, wc_ref), (yd_ref, wd_ref))):
        lo = n * D_MODEL
        gate = _sigmoid(gl_ref[:, lo:lo + D_MODEL].astype(F32) + gb_ref[:, lo:lo + D_MODEL])
        term = gate * _dot(y_ref[...], w_ref[...])
        merged = term if merged is None else merged + term
    o_ref[...] = x_ref[...] + _dot(merged.astype(BF16), wo_ref[...])


def _merge(x2, proj, gb, ys, ws, wo, tm):
    m = x2.shape[0]
    row = lambda width: pl.BlockSpec((tm, width), lambda i: (i, 0))
    wspec = pl.BlockSpec((D_MODEL, D_MODEL), lambda i: (0, 0))
    return pl.pallas_call(
        _merge_kernel,
        grid=(m // tm,),
        in_specs=[row(D_MODEL), row(N_BRANCH * D_MODEL),
                  pl.BlockSpec((1, N_BRANCH * D_MODEL), lambda i: (0, 0)),
                  row(D_MODEL), row(D_MODEL), row(D_MODEL), row(D_MODEL),
                  wspec, wspec, wspec, wspec, wspec],
        out_specs=row(D_MODEL),
        out_shape=jax.ShapeDtypeStruct((m, D_MODEL), F32),
        compiler_params=_cparams(("parallel",)),
        name="merge",
    )(x2, proj, gb, *ys, *ws, wo)


def _ffn_kernel(x_ref, g2_ref, w1_ref, w2_ref, g3_ref, wpg_ref, p_ref, wpe_ref, o_ref,
                xn_ref, acc_ref):
    j = pl.program_id(1)

    @pl.when(j == 0)
    def _():
        xn_ref[...] = _rms(x_ref[...], g2_ref[...]).astype(BF16)
        acc_ref[...] = jnp.zeros_like(acc_ref)

    h = jnp.maximum(_dot(xn_ref[...], w1_ref[...]), 0.0)
    acc_ref[...] += _dot((h * h).astype(BF16), w2_ref[...])

    @pl.when(j == pl.num_programs(1) - 1)
    def _():
        x = x_ref[...] + acc_ref[...]
        gate = _sigmoid(_dot(_rms(x, g3_ref[...]).astype(BF16), wpg_ref[...]))
        o_ref[...] = x + gate * _dot(p_ref[...].astype(BF16), wpe_ref[...])


def _ffn(x2, g2, w1, w2, g3, wpg, p2, wpe, tm, tf):
    m = x2.shape[0]
    dff = w1.shape[1]
    pdim = p2.shape[1]
    return pl.pallas_call(
        _ffn_kernel,
        grid=(m // tm, dff // tf),
        in_specs=[pl.BlockSpec((tm, D_MODEL), lambda i, j: (i, 0)),
                  pl.BlockSpec((1, D_MODEL), lambda i, j: (0, 0)),
                  pl.BlockSpec((D_MODEL, tf), lambda i, j: (0, j)),
                  pl.BlockSpec((tf, D_MODEL), lambda i, j: (j, 0)),
                  pl.BlockSpec((1, D_MODEL), lambda i, j: (0, 0)),
                  pl.BlockSpec((D_MODEL, D_MODEL), lambda i, j: (0, 0)),
                  pl.BlockSpec((tm, pdim), lambda i, j: (i, 0)),
                  pl.BlockSpec((pdim, D_MODEL), lambda i, j: (0, 0))],
        out_specs=pl.BlockSpec((tm, D_MODEL), lambda i, j: (i, 0)),
        out_shape=jax.ShapeDtypeStruct((m, D_MODEL), F32),
        scratch_shapes=[pltpu.VMEM((tm, D_MODEL), BF16), pltpu.VMEM((tm, D_MODEL), F32)],
        compiler_params=_cparams(("parallel", "arbitrary")),
        name="ffn_ple",
    )(x2, g2, w1, w2, g3, wpg, p2, wpe)


def _block_diag(blocks):
    n, c, d = blocks.shape
    eye = jnp.eye(n, dtype=blocks.dtype)
    return jnp.einsum('ncd,nm->ncmd', blocks, eye).reshape(n * c, n * d)


def _permute_w_in(w_in):
    sizes = (1024, 1024, 1024, 1024, 1024, 384, 256, 64, 512, 512, 1024, 1024, 4096)
    offs = [0]
    for s in sizes:
        offs.append(offs[-1] + s)
    (a_x, a_g, b_q, b_k, b_v, c_q, c_kv, c_kpe, d_q, d_k, d_v, d_g, gates) = [
        w_in[:, offs[n]:offs[n + 1]] for n in range(len(sizes))]
    pad = jnp.zeros((w_in.shape[0], C_WIDTH - 704), w_in.dtype)
    return jnp.concatenate([gates, a_x, a_g, b_q, b_k, b_v, d_v, d_g, d_q, d_k,
                            c_q, c_kv, c_kpe, pad], axis=1).astype(BF16)


def _lru_weights(wa, ba, wx, bx):
    w = MXU_DIM
    nt = D_MODEL // w
    dense = [_block_diag(m) for m in (wa[0], wx[0], wa[1], wx[1])]
    wd = jnp.stack([jnp.concatenate([d[c * w:(c + 1) * w, c * w:(c + 1) * w] for d in dense], axis=1)
                    for c in range(nt)]).astype(BF16)
    bias = (ba[0], bx[0], ba[1], bx[1])
    bd = jnp.stack([jnp.concatenate([b[c * w:(c + 1) * w] for b in bias])[None, :]
                    for c in range(nt)])
    return wd, bd


def _mla_weights(wuq, wukv, gq, gk):
    dh = MLA_NOPE + MLA_ROPE
    zpad = MLA_HEAD_PAD - dh
    wq = wuq.reshape(MLA_Q_LORA, MLA_HEADS, dh)
    wq = jnp.pad(wq, ((0, 0), (0, 0), (0, zpad))).reshape(MLA_Q_LORA, MLA_HEADS * MLA_HEAD_PAD)
    wkv = wukv.reshape(MLA_KV_LORA, MLA_HEADS, MLA_NOPE + MLA_V)
    wk = wkv[:, :, :MLA_NOPE].reshape(MLA_KV_LORA, MLA_HEADS * MLA_NOPE)
    wv = wkv[:, :, MLA_NOPE:].reshape(MLA_KV_LORA, MLA_HEADS * MLA_V)
    gqp = jnp.pad(gq, (0, zpad))[None, :]
    gkp = jnp.pad(gk, (0, zpad))[None, :]
    return wq.astype(BF16), jnp.concatenate([wk, wv], axis=1).astype(BF16), gqp, gkp


def kernel(x, p, norm1_g, w_in, gate_b, conv_w, conv_b, lru_wa, lru_ba, lru_wx, lru_bx, lru_lambda,
           diff_q_g, diff_k_g, diff_lam, diff_sub_g, mla_qa_g, mla_wuq, mla_kva_g, mla_wukv,
           mla_q_g, mla_k_g, ret_gn_g, w_br_a, w_br_b, w_br_c, w_br_d, w_out, norm2_g,
           w_ff1, w_ff2, norm3_g, w_ple_gate, w_ple_proj):
    batch, seq, d_model = x.shape
    depth = w_in.shape[0]
    assert d_model == D_MODEL and seq % (8 * N_SEG) == 0
    m = batch * seq
    tm_big = min(1024, seq)
    tm_small = min(512, seq)
    tq = min(512, seq)
    tk = min(512, seq)

    tabs_b = _rope_lane_tables(seq, DIFF_ROT, DIFF_DH, ROPE_THETA)
    tabs_c = _rope_lane_tables(seq, MLA_ROPE, LANES, ROPE_THETA)
    tabs_d = _rope_lane_tables(seq, RET_DK, RET_DK, RET_THETA)
    log_g = jnp.log1p(-jnp.exp2(-5.0 - jnp.arange(RET_HEADS, dtype=F32)))
    lg = jnp.broadcast_to(log_g.reshape(RET_HEADS // 2, 2, 1), (RET_HEADS // 2, 2, LANES))
    grp = jnp.arange(MXU_DIM) // DIFF_DH
    gm = ((grp[:, None] == grp[None, :]).astype(F32) / DIFF_DH).astype(BF16)
    ones_row = jnp.ones((1, LANES), F32)

    x2 = x.reshape(m, d_model)
    for i in range(depth):
        proj = _inproj(x2, norm1_g[i][None, :], _permute_w_in(w_in[i]), tm_big, C_WIDTH)

        wd, bd = _lru_weights(lru_wa[i], lru_ba[i], lru_wx[i], lru_bx[i])
        y_a = _rglru(proj, conv_w[i], conv_b[i][None, :], wd, bd, lru_lambda[i], batch, seq)

        lam_init = 0.8 - 0.6 * math.exp(-0.3 * i)
        lp = diff_lam[i].astype(F32)
        lam = jnp.exp(jnp.sum(lp[0] * lp[1])) - jnp.exp(jnp.sum(lp[2] * lp[3])) + lam_init
        gq = jnp.tile(diff_q_g[i], MXU_DIM // DIFF_DH)[None, :]
        gk = jnp.tile(diff_k_g[i], MXU_DIM // DIFF_DH)[None, :]
        qb, kb = _diff_prep(proj, gq, gk, tabs_b, gm, seq, tm_small, DIFF_DH ** -0.5 * LOG2E)
        attn_b = functools.partial(_attention, v_off=OFF_BV, batch=batch, seq=seq, heads=DIFF_HEADS,
                                   dq=LANES, tq=tq, tk=tk, diff=True, out_scale=1.0 - lam_init)
        y_b = lax.cond(
            _score_bound(diff_q_g[i], diff_k_g[i], DIFF_DH) <= MAX_SCORE_LOG2,
            functools.partial(attn_b, bounded=True), functools.partial(attn_b, bounded=False),
            qb, kb, proj, jnp.full((1, LANES), lam, F32), diff_sub_g[i][None, :])

        wq, wkv, gqp, gkp = _mla_weights(mla_wuq[i], mla_wukv[i], mla_q_g[i], mla_k_g[i])
        qc, kc, vc = _mla_prep(proj, mla_qa_g[i][None, :], wq, mla_kva_g[i][None, :], wkv,
                               gqp, gkp, tabs_c, seq, tm_small,
                               (MLA_NOPE + MLA_ROPE) ** -0.5 * LOG2E)
        attn_c = functools.partial(_attention, v_off=0, batch=batch, seq=seq, heads=MLA_HEADS,
                                   dq=MLA_HEAD_PAD, tq=tq, tk=tk, diff=False, out_scale=1.0)
        y_c = lax.cond(
            _score_bound(mla_q_g[i], mla_k_g[i], MLA_NOPE + MLA_ROPE) <= MAX_SCORE_LOG2,
            functools.partial(attn_c, bounded=True), functools.partial(attn_c, bounded=False),
            qc, kc, vc, ones_row, ones_row)

        y_d = _retention(proj, lg, ret_gn_g[i][None, :], tabs_d, batch, seq)

        x2 = _merge(x2, proj, gate_b[i].reshape(1, N_BRANCH * D_MODEL), (y_a, y_b, y_c, y_d),
                    tuple(w[i].astype(BF16) for w in (w_br_a, w_br_b, w_br_c, w_br_d)),
                    w_out[i].astype(BF16), min(256, seq))

        x2 = _ffn(x2, norm2_g[i][None, :], w_ff1[i].astype(BF16), w_ff2[i].astype(BF16),
                  norm3_g[i][None, :], w_ple_gate[i].astype(BF16),
                  p[i].reshape(m, p.shape[-1]), w_ple_proj[i].astype(BF16), tm_big, 512)
    return x2.reshape(batch, seq, d_model)
```

```python
import functools
import math

import jax
import jax.numpy as jnp
from jax import lax
from jax.experimental import pallas as pl
from jax.experimental.pallas import tpu as pltpu

F32 = jnp.float32
BF16 = jnp.bfloat16

EPS = 1e-6
ROPE_THETA = 500000.0
RET_THETA = 10000.0
LRU_C = 8.0
LOG2E = 1.4426950408889634

LANES = 128
MXU_DIM = 256
VMEM_LIMIT = 56 * 1024 * 1024

D_MODEL = 1024
N_BRANCH = 4
LRU_BLOCKS = 16
CONV_WIDTH = 4
DIFF_HEADS = 8
DIFF_DH = 64
DIFF_ROT = 16
MLA_HEADS = 8
MLA_NOPE = 128
MLA_ROPE = 64
MLA_V = 128
MLA_Q_LORA = 384
MLA_KV_LORA = 256
MLA_HEAD_PAD = 256
RET_HEADS = 8
RET_DK = 64
RET_DV = 128
RET_CHUNK = 256
ATTN_ROW_BLOCK = 128
MAX_SCORE_LOG2 = 60.0
N_SEG = 8
SEG_PAD = 4

OFF_GATE = 0
OFF_AX = 4096
OFF_AG = 5120
OFF_BQ = 6144
OFF_BK = 7168
OFF_BV = 8192
OFF_DV = 9216
OFF_DG = 10240
OFF_DQ = 11264
OFF_DK = 11776
OFF_C = 12288
C_WIDTH = 768
N_PROJ = OFF_C + C_WIDTH


def _cparams(sem):
    return pltpu.CompilerParams(dimension_semantics=sem, vmem_limit_bytes=VMEM_LIMIT)


def _sigmoid(x):
    return 1.0 / (1.0 + jnp.exp(-x))


def _rms(x, g):
    return x * lax.rsqrt(jnp.mean(x * x, axis=-1, keepdims=True) + EPS) * g


def _dot(a, b):
    return jnp.dot(a, b, preferred_element_type=F32)


def _dot_nt(a, b):
    return lax.dot_general(a, b, (((1,), (1,)), ((), ())), preferred_element_type=F32)


def _dot_tn(a, b):
    return lax.dot_general(a, b, (((0,), (0,)), ((), ())), preferred_element_type=F32)


def _inproj_kernel(x_ref, g_ref, w_ref, o_ref, xn_ref):
    @pl.when(pl.program_id(1) == 0)
    def _():
        xn_ref[...] = _rms(x_ref[...], g_ref[...]).astype(BF16)

    o_ref[...] = _dot(xn_ref[...], w_ref[...]).astype(o_ref.dtype)


def _inproj(x2, g, w, tm, tn):
    m, k = x2.shape
    n = w.shape[1]
    return pl.pallas_call(
        _inproj_kernel,
        grid=(m // tm, n // tn),
        in_specs=[pl.BlockSpec((tm, k), lambda i, j: (i, 0)),
                  pl.BlockSpec((1, k), lambda i, j: (0, 0)),
                  pl.BlockSpec((k, tn), lambda i, j: (0, j))],
        out_specs=pl.BlockSpec((tm, tn), lambda i, j: (i, j)),
        out_shape=jax.ShapeDtypeStruct((m, n), BF16),
        scratch_shapes=[pltpu.VMEM((tm, k), BF16)],
        compiler_params=_cparams(("parallel", "arbitrary")),
        name="inproj",
    )(x2, g, w)


def _gelu_tanh(x):
    return 0.5 * x * (1.0 + jnp.tanh(0.7978845608028654 * (x + 0.044715 * x * x * x)))


def _rglru_kernel(x_ref, ga_ref, cw_ref, cb_ref, wd_ref, bd_ref, lam_ref, o_ref,
                  xpad, a0, u0, a1, u1, *, seq):
    w = MXU_DIM
    nslab = w // LANES
    seg = seq // N_SEG
    pitch = seg + SEG_PAD
    left = CONV_WIDTH // 2

    xpad[0:8, :] = jnp.zeros((8, w), F32)
    xpad[seq + 8:seq + 16, :] = jnp.zeros((8, w), F32)
    for s in range(N_SEG):
        xpad[8 + s * seg:8 + (s + 1) * seg, :] = x_ref[s * seg:(s + 1) * seg, :].astype(F32)

    lam = lam_ref[...]
    nlam = -lam
    nls = -LRU_C * (jnp.maximum(nlam, 0.0) + jnp.log(1.0 + jnp.exp(-jnp.abs(nlam))))

    a_scr = (a0, a1)
    u_scr = (u0, u1)
    for s in range(N_SEG):
        base = 8 + s * seg - left
        xc = cb_ref[...]
        for t in range(CONV_WIDTH):
            xc = xc + xpad[base + t:base + t + seg, :] * cw_ref[t:t + 1, :]
        z = _dot(xc.astype(BF16), wd_ref[0]) + bd_ref[0]
        for d in range(2):
            r = _sigmoid(z[:, (2 * d) * w:(2 * d + 1) * w])
            gi = _sigmoid(z[:, (2 * d + 1) * w:(2 * d + 2) * w])
            log_a = nls[d:d + 1, :] * r
            a = jnp.exp(log_a)
            v = 1.0 - a * a
            u = (v * lax.rsqrt(jnp.maximum(v, 1e-30))) * (gi * xc)
            for sl in range(nslab):
                a_scr[d][sl, s * pitch:s * pitch + seg, :] = a[:, sl * LANES:(sl + 1) * LANES]
                u_scr[d][sl, s * pitch:s * pitch + seg, :] = u[:, sl * LANES:(sl + 1) * LANES]

    def rows(t):
        return pl.ds(t, N_SEG, stride=pitch)

    def scan_body(t, carry):
        new = []
        tb = seg - 1 - t
        for sl in range(nslab):
            hf, pf, hb, pb = carry[sl]
            af = a0[sl, rows(t), :]
            hf = af * hf + u0[sl, rows(t), :]
            pf = pf * af
            u0[sl, rows(t), :] = hf
            a0[sl, rows(t), :] = pf
            ab = a1[sl, rows(tb), :]
            hb = ab * hb + u1[sl, rows(tb), :]
            pb = pb * ab
            u1[sl, rows(tb), :] = hb
            a1[sl, rows(tb), :] = pb
            new.append((hf, pf, hb, pb))
        return tuple(new)

    zero = jnp.zeros((N_SEG, LANES), F32)
    one = jnp.ones((N_SEG, LANES), F32)
    fin = lax.fori_loop(0, seg, scan_body, tuple((zero, one, zero, one) for _ in range(nslab)))

    cf, cb = [], []
    for sl in range(nslab):
        hf, pf, hb, pb = fin[sl]
        c = jnp.zeros((1, LANES), F32)
        cfr = [c]
        for s in range(1, N_SEG):
            c = pf[s - 1:s, :] * c + hf[s - 1:s, :]
            cfr.append(c)
        cf.append(jnp.concatenate(cfr, axis=0))
        c = jnp.zeros((1, LANES), F32)
        cbr = [c]
        for s in range(N_SEG - 2, -1, -1):
            c = pb[s + 1:s + 2, :] * c + hb[s + 1:s + 2, :]
            cbr.append(c)
        cb.append(jnp.concatenate(cbr[::-1], axis=0))

    def fix_body(t, carry):
        for sl in range(nslab):
            hf = u0[sl, rows(t), :] + a0[sl, rows(t), :] * cf[sl]
            hb = u1[sl, rows(t), :] + a1[sl, rows(t), :] * cb[sl]
            u0[sl, rows(t), :] = hf + hb
        return carry

    lax.fori_loop(0, seg, fix_body, 0)

    for s in range(N_SEG):
        g = _gelu_tanh(ga_ref[s * seg:(s + 1) * seg, :].astype(F32))
        for sl in range(nslab):
            h = u0[sl, s * pitch:s * pitch + seg, :]
            o_ref[s * seg:(s + 1) * seg, sl * LANES:(sl + 1) * LANES] = (
                h * g[:, sl * LANES:(sl + 1) * LANES]).astype(o_ref.dtype)


def _rglru(proj, cw, cb, wd, bd, lam, batch, seq):
    w = MXU_DIM
    nt = D_MODEL // w
    nslab = w // LANES
    rows = N_SEG * (seq // N_SEG + SEG_PAD)
    scan_scratch = pltpu.VMEM((nslab, rows, LANES), F32)
    return pl.pallas_call(
        functools.partial(_rglru_kernel, seq=seq),
        grid=(batch, nt),
        in_specs=[pl.BlockSpec((seq, w), lambda b, c: (b, OFF_AX // w + c)),
                  pl.BlockSpec((seq, w), lambda b, c: (b, OFF_AG // w + c)),
                  pl.BlockSpec((CONV_WIDTH, w), lambda b, c: (0, c)),
                  pl.BlockSpec((1, w), lambda b, c: (0, c)),
                  pl.BlockSpec((1, w, 4 * w), lambda b, c: (c, 0, 0)),
                  pl.BlockSpec((1, 1, 4 * w), lambda b, c: (c, 0, 0)),
                  pl.BlockSpec((2, w), lambda b, c: (0, c))],
        out_specs=pl.BlockSpec((seq, w), lambda b, c: (b, c)),
        out_shape=jax.ShapeDtypeStruct((batch * seq, D_MODEL), BF16),
        scratch_shapes=[pltpu.VMEM((seq + 16, w), F32),
                        scan_scratch, scan_scratch, scan_scratch, scan_scratch],
        compiler_params=_cparams(("parallel", "parallel")),
        name="rglru",
    )(proj, proj, cw, cb, wd, bd, lam)


def _rope_block(x, c, sa, sb, half):
    return (x * c + pltpu.roll(x, LANES - half, axis=1) * sa
            + pltpu.roll(x, half, axis=1) * sb)


def _rope_mxu(x, perm, c, s):
    return x * c + _dot(x.astype(BF16), perm) * s


def _rope_perm(width, rot, period):
    half = rot // 2
    lane = jnp.arange(width)
    j = lane % period
    src = jnp.where(j < half, lane + half, lane - half)
    sign = jnp.where(j < half, -1.0, 1.0) * (j < rot)
    return ((lane[:, None] == src[None, :]) * sign[None, :]).astype(BF16)


def _rope_cs_tables(seq, rot, period, theta):
    pos = jnp.arange(seq, dtype=F32)
    inv_freq = theta ** (-jnp.arange(0, rot, 2, dtype=F32) / rot)
    ang = pos[:, None] * inv_freq[None, :]
    cos, sin = jnp.cos(ang), jnp.sin(ang)
    pad = period - rot
    c = jnp.concatenate([cos, cos, jnp.ones((seq, pad), F32)], axis=1)
    s = jnp.concatenate([sin, sin, jnp.zeros((seq, pad), F32)], axis=1)
    rep = LANES // period
    return jnp.tile(c, (1, rep)), jnp.tile(s, (1, rep))


def _rope_lane_tables(seq, rot, period, theta):
    half = rot // 2
    pos = jnp.arange(seq, dtype=F32)
    inv_freq = theta ** (-jnp.arange(0, rot, 2, dtype=F32) / rot)
    ang = pos[:, None] * inv_freq[None, :]
    cos, sin = jnp.cos(ang), jnp.sin(ang)
    pad = period - rot
    c1 = jnp.concatenate([cos, cos, jnp.ones((seq, pad), F32)], axis=1)
    s1 = jnp.concatenate([-sin, jnp.zeros((seq, half + pad), F32)], axis=1)
    s2 = jnp.concatenate([jnp.zeros((seq, half), F32), sin, jnp.zeros((seq, pad), F32)], axis=1)
    rep = LANES // period
    return jnp.tile(c1, (1, rep)), jnp.tile(s1, (1, rep)), jnp.tile(s2, (1, rep))


def _diff_prep_kernel(q_ref, k_ref, gq_ref, gk_ref, c_ref, s_ref, gm_ref, rp_ref,
                      qo_ref, ko_ref, *, qscale):
    c = jnp.concatenate([c_ref[...]] * (MXU_DIM // LANES), axis=1)
    sn = jnp.concatenate([s_ref[...]] * (MXU_DIM // LANES), axis=1)
    gm = gm_ref[...]
    rp = rp_ref[...]
    for src, g_ref, is_key in ((q_ref, gq_ref, False), (k_ref, gk_ref, True)):
        for blk in range(D_MODEL // MXU_DIM):
            x = src[:, blk * MXU_DIM:(blk + 1) * MXU_DIM].astype(F32)
            ms = _dot((x * x).astype(BF16), gm)
            xr = _rope_mxu(x * lax.rsqrt(ms + EPS) * g_ref[...], rp, c, sn)
            for h in range(MXU_DIM // LANES):
                xb = xr[:, h * LANES:(h + 1) * LANES]
                lo = blk * MXU_DIM + h * LANES
                if is_key:
                    ko_ref[lo:lo + LANES, :] = xb.T.astype(ko_ref.dtype)
                else:
                    qo_ref[:, lo:lo + LANES] = (xb * qscale).astype(qo_ref.dtype)


def _diff_prep(proj, gq, gk, tabs, gm, rp, seq, tm, qscale):
    m = proj.shape[0]
    nrow = seq // tm
    tab_spec = pl.BlockSpec((tm, LANES), lambda i: (i % nrow, 0))
    return pl.pallas_call(
        functools.partial(_diff_prep_kernel, qscale=qscale),
        grid=(m // tm,),
        in_specs=[pl.BlockSpec((tm, D_MODEL), lambda i: (i, OFF_BQ // D_MODEL)),
                  pl.BlockSpec((tm, D_MODEL), lambda i: (i, OFF_BK // D_MODEL)),
                  pl.BlockSpec((1, MXU_DIM), lambda i: (0, 0)),
                  pl.BlockSpec((1, MXU_DIM), lambda i: (0, 0)),
                  tab_spec, tab_spec,
                  pl.BlockSpec((MXU_DIM, MXU_DIM), lambda i: (0, 0)),
                  pl.BlockSpec((MXU_DIM, MXU_DIM), lambda i: (0, 0))],
        out_specs=[pl.BlockSpec((tm, D_MODEL), lambda i: (i, 0)),
                   pl.BlockSpec((None, D_MODEL, tm), lambda i: (i // nrow, 0, i % nrow))],
        out_shape=[jax.ShapeDtypeStruct((m, D_MODEL), BF16),
                   jax.ShapeDtypeStruct((m // seq, D_MODEL, seq), BF16)],
        compiler_params=_cparams(("parallel",)),
        name="diff_prep",
    )(proj, proj, gq, gk, *tabs, gm, rp)


def _mla_prep_kernel(c_ref, qag_ref, wuq_ref, kvag_ref, wukv_ref, gq_ref, gk_ref,
                     c_tab, s_tab, rp_ref, qo_ref, ko_ref, vo_ref, *, qscale):
    dh = MLA_NOPE + MLA_ROPE
    ct, st, rp = c_tab[...], s_tab[...], rp_ref[...]

    def sumsq(x):
        return _dot((x * x).astype(BF16), jnp.ones((x.shape[1], LANES), BF16))

    def rms_wide(x, g):
        inv = lax.rsqrt(sumsq(x) * (1.0 / x.shape[1]) + EPS)
        return x * jnp.concatenate([inv] * (x.shape[1] // LANES), axis=1) * g

    cq = c_ref[:, 0:MLA_Q_LORA].astype(F32)
    ckv = c_ref[:, MLA_Q_LORA:MLA_Q_LORA + MLA_KV_LORA].astype(F32)
    kpe = c_ref[:, MLA_Q_LORA + MLA_KV_LORA:C_WIDTH].astype(F32)
    q = _dot(rms_wide(cq, qag_ref[...]).astype(BF16), wuq_ref[...])
    kv = _dot(rms_wide(ckv, kvag_ref[...]).astype(BF16), wukv_ref[...])
    gq_n, gq_r = gq_ref[:, 0:LANES], gq_ref[:, LANES:2 * LANES]
    gk_n, gk_r = gk_ref[:, 0:LANES], gk_ref[:, LANES:2 * LANES]
    kpe_ss = sumsq(kpe)
    for h in range(MLA_HEADS):
        lo = h * MLA_HEAD_PAD
        qn, qr = q[:, lo:lo + LANES], q[:, lo + LANES:lo + 2 * LANES]
        qh = q[:, lo:lo + 2 * LANES]
        inv = lax.rsqrt(jnp.sum(qh * qh, axis=-1, keepdims=True) * (1.0 / dh) + EPS) * qscale
        qo_ref[:, lo:lo + LANES] = (qn * inv * gq_n).astype(qo_ref.dtype)
        qo_ref[:, lo + LANES:lo + 2 * LANES] = _rope_mxu(
            qr * inv * gq_r, rp, ct, st).astype(qo_ref.dtype)
        kn = kv[:, h * LANES:(h + 1) * LANES]
        inv = lax.rsqrt((jnp.sum(kn * kn, axis=-1, keepdims=True) + kpe_ss) * (1.0 / dh) + EPS)
        ko_ref[lo:lo + LANES, :] = (kn * inv * gk_n).T.astype(ko_ref.dtype)
        ko_ref[lo + LANES:lo + 2 * LANES, :] = _rope_mxu(
            kpe * inv * gk_r, rp, ct, st).T.astype(ko_ref.dtype)
    vo_ref[...] = kv[:, MLA_HEADS * LANES:].astype(vo_ref.dtype)


def _mla_prep(proj, qag, wuq, kvag, wukv, gq, gk, tabs, rp, seq, tm, qscale):
    m = proj.shape[0]
    nrow = seq // tm
    hp = MLA_HEADS * MLA_HEAD_PAD
    tab_spec = pl.BlockSpec((tm, LANES), lambda i: (i % nrow, 0))
    const = lambda shape: pl.BlockSpec(shape, lambda i: (0, 0))
    return pl.pallas_call(
        functools.partial(_mla_prep_kernel, qscale=qscale),
        grid=(m // tm,),
        in_specs=[pl.BlockSpec((tm, C_WIDTH), lambda i: (i, OFF_C // C_WIDTH)),
                  const((1, MLA_Q_LORA)), const((MLA_Q_LORA, hp)),
                  const((1, MLA_KV_LORA)), const((MLA_KV_LORA, 2 * MLA_HEADS * LANES)),
                  const((1, MLA_HEAD_PAD)), const((1, MLA_HEAD_PAD)),
                  tab_spec, tab_spec, const((LANES, LANES))],
        out_specs=[pl.BlockSpec((tm, hp), lambda i: (i, 0)),
                   pl.BlockSpec((None, hp, tm), lambda i: (i // nrow, 0, i % nrow)),
                   pl.BlockSpec((tm, MLA_HEADS * MLA_V), lambda i: (i, 0))],
        out_shape=[jax.ShapeDtypeStruct((m, hp), BF16),
                   jax.ShapeDtypeStruct((m // seq, hp, seq), BF16),
                   jax.ShapeDtypeStruct((m, MLA_HEADS * MLA_V), BF16)],
        compiler_params=_cparams(("parallel",)),
        name="mla_prep",
    )(proj, qag, wuq, kvag, wukv, gq, gk, *tabs, rp)


def _attn_kernel(q_ref, k_ref, v_ref, lam_ref, sg_ref, o_ref, vext, m_scr, acc_scr,
                 *, seq, tk, diff, out_scale):
    tq = q_ref.shape[0]

    @pl.when(pl.program_id(2) == 0)
    def _():
        vext[:, 0:LANES] = v_ref[...]
        vext[:, LANES:2 * LANES] = jnp.ones((seq, LANES), BF16)

    q = q_ref[...]
    if diff:
        lane = lax.broadcasted_iota(jnp.int32, q.shape, 1)
        zero = jnp.zeros_like(q)
        q = jnp.concatenate([jnp.where(lane < DIFF_DH, q, zero),
                             jnp.where(lane >= DIFF_DH, q, zero)], axis=0)
    rows = q.shape[0]
    m_scr[...] = jnp.full((rows, LANES), -1e30, F32)
    acc_scr[...] = jnp.zeros((rows, 2 * LANES), F32)

    def body(j, carry):
        off = pl.multiple_of(j * tk, tk)
        s = _dot(q, k_ref[:, pl.ds(off, tk)])
        m_prev = m_scr[...]
        m_new = jnp.maximum(m_prev, jnp.max(s, axis=1, keepdims=True))
        alpha = jnp.exp2(m_prev - m_new)
        p = jnp.exp2(s - m_new[:, 0:1]).astype(BF16)
        acc_scr[...] = (acc_scr[...] * jnp.concatenate([alpha, alpha], axis=1)
                        + _dot(p, vext[pl.ds(off, tk), :]))
        m_scr[...] = m_new
        return carry

    lax.fori_loop(0, seq // tk, body, 0)

    acc = acc_scr[...]
    o = acc[:, 0:LANES] / acc[:, LANES:2 * LANES]
    if diff:
        o = o[0:tq, :] - lam_ref[...] * o[tq:2 * tq, :]
        o = _rms(o, sg_ref[...]) * out_scale
    o_ref[...] = o.astype(o_ref.dtype)


def _attn_bounded_kernel(q_ref, k_ref, v_ref, lam_ref, sg_ref, o_ref, vext, m_scr, acc_scr,
                         *, seq, tk, diff, out_scale):
    del m_scr
    tq = q_ref.shape[0]

    @pl.when(pl.program_id(2) == 0)
    def _():
        vext[:, 0:LANES] = v_ref[...]
        vext[:, LANES:2 * LANES] = jnp.ones((seq, LANES), BF16)

    q = q_ref[...]
    if diff:
        lane = lax.broadcasted_iota(jnp.int32, q.shape, 1)
        zero = jnp.zeros_like(q)
        q = jnp.concatenate([jnp.where(lane < DIFF_DH, q, zero),
                             jnp.where(lane >= DIFF_DH, q, zero)], axis=0)
    rows = q.shape[0]
    rb = ATTN_ROW_BLOCK
    acc_scr[...] = jnp.zeros((rows, 2 * LANES), F32)

    def body(j, carry):
        off = pl.multiple_of(j * tk, tk)
        kt = k_ref[:, pl.ds(off, tk)]
        vt = vext[pl.ds(off, tk), :]
        for r in range(rows // rb):
            p = jnp.exp2(_dot(q[r * rb:(r + 1) * rb, :], kt)).astype(BF16)
            acc_scr[r * rb:(r + 1) * rb, :] += _dot(p, vt)
        return carry

    lax.fori_loop(0, seq // tk, body, 0, unroll=True)

    acc = acc_scr[...]
    o = acc[:, 0:LANES] / acc[:, LANES:2 * LANES]
    if diff:
        o = o[0:tq, :] - lam_ref[...] * o[tq:2 * tq, :]
        o = _rms(o, sg_ref[...]) * out_scale
    o_ref[...] = o.astype(o_ref.dtype)


def _score_bound(gq, gk, dh):
    return 1.02 * LOG2E * math.sqrt(dh) * jnp.max(jnp.abs(gq)) * jnp.max(jnp.abs(gk))


def _attention(q, k, v, lam, sg, *, v_off, batch, seq, heads, dq, tq, tk, diff, out_scale, bounded):
    nq = seq // tq
    rows = 2 * tq if diff else tq
    vblk = v_off // LANES
    body = _attn_bounded_kernel if bounded else _attn_kernel
    return pl.pallas_call(
        functools.partial(body, seq=seq, tk=tk, diff=diff, out_scale=out_scale),
        grid=(batch, heads, nq),
        in_specs=[pl.BlockSpec((tq, dq), lambda b, h, i: (b * nq + i, h)),
                  pl.BlockSpec((None, dq, seq), lambda b, h, i: (b, h, 0)),
                  pl.BlockSpec((seq, LANES), lambda b, h, i: (b, vblk + h)),
                  pl.BlockSpec((1, LANES), lambda b, h, i: (0, 0)),
                  pl.BlockSpec((1, LANES), lambda b, h, i: (0, 0))],
        out_specs=pl.BlockSpec((tq, LANES), lambda b, h, i: (b * nq + i, h)),
        out_shape=jax.ShapeDtypeStruct((batch * seq, heads * LANES), BF16),
        scratch_shapes=[pltpu.VMEM((seq, 2 * LANES), BF16),
                        pltpu.VMEM((rows, LANES), F32),
                        pltpu.VMEM((rows, 2 * LANES), F32)],
        compiler_params=_cparams(("parallel", "parallel", "arbitrary")),
        name=("diff_attn" if diff else "mla_attn") + ("_bounded" if bounded else "_online"),
    )(q, k, v, lam, sg)


def _retention_kernel(q_ref, k_ref, v_ref, g_ref, lg_ref, gn_ref, c_tab, sa_tab, sb_tab,
                      o_ref, qs, kts, o_fwd, o_bwd, state, *, seq):
    ch = RET_CHUNK if seq % RET_CHUNK == 0 else seq
    nc = seq // ch
    half = RET_DK // 2
    ct, sat, sbt = c_tab[...], sa_tab[...], sb_tab[...]
    qs[...] = _rope_block(q_ref[...].astype(F32), ct, sat, sbt, half)
    kt = (_rope_block(k_ref[...].astype(F32), ct, sat, sbt, half) * (RET_DK ** -0.5)).T
    row = lax.broadcasted_iota(jnp.int32, kt.shape, 0)
    kts[0] = jnp.where(row < RET_DK, kt, 0.0)
    kts[1] = jnp.where(row >= RET_DK, kt, 0.0)
    state[...] = jnp.zeros(state.shape, F32)

    ri = lax.broadcasted_iota(jnp.int32, (ch, ch), 0).astype(F32)
    ci = lax.broadcasted_iota(jnp.int32, (ch, ch), 1).astype(F32)
    qidx = lax.broadcasted_iota(jnp.int32, (ch, 1), 0).astype(F32)
    kidx = lax.broadcasted_iota(jnp.int32, (1, ch), 1).astype(F32)
    dec = []
    for hh in range(2):
        lg = lg_ref[hh:hh + 1, 0:1]
        dec.append(dict(
            intra=jnp.exp(lg * jnp.abs(ri - ci)),
            tail=jnp.exp(lg * (ch - 1.0 - kidx)),
            head=jnp.exp(lg * kidx),
            qf=jnp.exp(lg * (qidx + 1.0)),
            qb=jnp.exp(lg * (ch - qidx)),
            chunk=jnp.exp(lg * float(ch))))

    def sweep(t, carry):
        of = pl.multiple_of(t * ch, ch)
        ob = pl.multiple_of((nc - 1 - t) * ch, ch)
        qf = qs[pl.ds(of, ch), :]
        qb = qs[pl.ds(ob, ch), :]
        for hh in range(2):
            d = dec[hh]
            vlo = hh * RET_DV
            ktf = kts[hh, :, pl.ds(of, ch)]
            vf = v_ref[pl.ds(of, ch), vlo:vlo + RET_DV]
            sc = _dot(qf.astype(BF16), ktf.astype(BF16)) * d["intra"]
            past = state[2 * hh]
            o_fwd[hh, pl.ds(of, ch), :] = (_dot(sc.astype(BF16), vf)
                                           + _dot((qf * d["qf"]).astype(BF16), past.astype(BF16)))
            state[2 * hh] = past * d["chunk"] + _dot((ktf * d["tail"]).astype(BF16), vf)
            ktb = kts[hh, :, pl.ds(ob, ch)]
            vb = v_ref[pl.ds(ob, ch), vlo:vlo + RET_DV]
            fut = state[2 * hh + 1]
            o_bwd[hh, pl.ds(ob, ch), :] = _dot((qb * d["qb"]).astype(BF16), fut.astype(BF16))
            state[2 * hh + 1] = fut * d["chunk"] + _dot((ktb * d["head"]).astype(BF16), vb)
        return carry

    lax.fori_loop(0, nc, sweep, 0, unroll=2 if nc % 2 == 0 else 1)

    lane_mean = jnp.full((RET_DV, RET_DV), 1.0 / RET_DV, BF16)

    def fin(n, carry):
        off = pl.multiple_of(n * ch, ch)
        for hh in range(2):
            vlo = hh * RET_DV
            o = o_fwd[hh, pl.ds(off, ch), :] + o_bwd[hh, pl.ds(off, ch), :]
            oc = o - _dot(o.astype(BF16), lane_mean)
            var = _dot((oc * oc).astype(BF16), lane_mean)
            y = oc * lax.rsqrt(var + EPS) * gn_ref[...]
            g = g_ref[pl.ds(off, ch), vlo:vlo + RET_DV].astype(F32)
            o_ref[pl.ds(off, ch), vlo:vlo + RET_DV] = (g * _sigmoid(g) * y).astype(o_ref.dtype)
        return carry

    lax.fori_loop(0, nc, fin, 0, unroll=2 if nc % 2 == 0 else 1)


def _retention(proj, lg, gn, tabs, batch, seq):
    npair = RET_HEADS // 2
    pw = 2 * RET_DV
    tab_spec = pl.BlockSpec((seq, LANES), lambda b, j: (0, 0))
    return pl.pallas_call(
        functools.partial(_retention_kernel, seq=seq),
        grid=(batch, npair),
        in_specs=[pl.BlockSpec((seq, LANES), lambda b, j: (b, OFF_DQ // LANES + j)),
                  pl.BlockSpec((seq, LANES), lambda b, j: (b, OFF_DK // LANES + j)),
                  pl.BlockSpec((seq, pw), lambda b, j: (b, OFF_DV // pw + j)),
                  pl.BlockSpec((seq, pw), lambda b, j: (b, OFF_DG // pw + j)),
                  pl.BlockSpec((None, 2, LANES), lambda b, j: (j, 0, 0)),
                  pl.BlockSpec((1, RET_DV), lambda b, j: (0, 0)),
                  tab_spec, tab_spec, tab_spec],
        out_specs=pl.BlockSpec((seq, pw), lambda b, j: (b, j)),
        out_shape=jax.ShapeDtypeStruct((batch * seq, RET_HEADS * RET_DV), BF16),
        scratch_shapes=[pltpu.VMEM((seq, LANES), F32),
                        pltpu.VMEM((2, LANES, seq), F32),
                        pltpu.VMEM((2, seq, RET_DV), F32),
                        pltpu.VMEM((2, seq, RET_DV), F32),
                        pltpu.VMEM((4, LANES, RET_DV), F32)],
        compiler_params=_cparams(("parallel", "parallel")),
        name="retention",
    )(proj, proj, proj, proj, lg, gn, *tabs)


def _merge_kernel(x_ref, gl_ref, gb_ref, ya_ref, yb_ref, yc_ref, yd_ref,
                  wa_ref, wb_ref, wc_ref, wd_ref, wo_ref, o_ref):
    merged = None
    for n, (y_ref, w_ref) in enumerate(((ya_ref, wa_ref), (yb_ref, wb_ref),
                                        (yc_ref, wc_ref), (yd_ref, wd_ref))):
        lo = n * D_MODEL
        gate = _sigmoid(gl_ref[:, lo:lo + D_MODEL].astype(F32) + gb_ref[:, lo:lo + D_MODEL])
        term = gate * _dot(y_ref[...], w_ref[...])
        merged = term if merged is None else merged + term
    o_ref[...] = x_ref[...] + _dot(merged.astype(BF16), wo_ref[...])


def _merge(x2, proj, gb, ys, ws, wo, tm):
    m = x2.shape[0]
    row = lambda width: pl.BlockSpec((tm, width), lambda i: (i, 0))
    wspec = pl.BlockSpec((D_MODEL, D_MODEL), lambda i: (0, 0))
    return pl.pallas_call(
        _merge_kernel,
        grid=(m // tm,),
        in_specs=[row(D_MODEL), row(N_BRANCH * D_MODEL),
                  pl.BlockSpec((1, N_BRANCH * D_MODEL), lambda i: (0, 0)),
                  row(D_MODEL), row(D_MODEL), row(D_MODEL), row(D_MODEL),
                  wspec, wspec, wspec, wspec, wspec],
        out_specs=row(D_MODEL),
        out_shape=jax.ShapeDtypeStruct((m, D_MODEL), F32),
        compiler_params=_cparams(("parallel",)),
        name="merge",
    )(x2, proj, gb, *ys, *ws, wo)


def _ffn_kernel(x_ref, g2_ref, w1_ref, w2_ref, g3_ref, wpg_ref, p_ref, wpe_ref, o_ref,
                xn_ref, acc_ref):
    j = pl.program_id(1)

    @pl.when(j == 0)
    def _():
        xn_ref[...] = _rms(x_ref[...], g2_ref[...]).astype(BF16)
        acc_ref[...] = jnp.zeros_like(acc_ref)

    h = jnp.maximum(_dot(xn_ref[...], w1_ref[...]), 0.0)
    acc_ref[...] += _dot((h * h).astype(BF16), w2_ref[...])

    @pl.when(j == pl.num_programs(1) - 1)
    def _():
        x = x_ref[...] + acc_ref[...]
        gate = _sigmoid(_dot(_rms(x, g3_ref[...]).astype(BF16), wpg_ref[...]))
        o_ref[...] = x + gate * _dot(p_ref[...].astype(BF16), wpe_ref[...])


def _ffn(x2, g2, w1, w2, g3, wpg, p2, wpe, tm, tf):
    m = x2.shape[0]
    dff = w1.shape[1]
    pdim = p2.shape[1]
    return pl.pallas_call(
        _ffn_kernel,
        grid=(m // tm, dff // tf),
        in_specs=[pl.BlockSpec((tm, D_MODEL), lambda i, j: (i, 0)),
                  pl.BlockSpec((1, D_MODEL), lambda i, j: (0, 0)),
                  pl.BlockSpec((D_MODEL, tf), lambda i, j: (0, j)),
                  pl.BlockSpec((tf, D_MODEL), lambda i, j: (j, 0)),
                  pl.BlockSpec((1, D_MODEL), lambda i, j: (0, 0)),
                  pl.BlockSpec((D_MODEL, D_MODEL), lambda i, j: (0, 0)),
                  pl.BlockSpec((tm, pdim), lambda i, j: (i, 0)),
                  pl.BlockSpec((pdim, D_MODEL), lambda i, j: (0, 0))],
        out_specs=pl.BlockSpec((tm, D_MODEL), lambda i, j: (i, 0)),
        out_shape=jax.ShapeDtypeStruct((m, D_MODEL), F32),
        scratch_shapes=[pltpu.VMEM((tm, D_MODEL), BF16), pltpu.VMEM((tm, D_MODEL), F32)],
        compiler_params=_cparams(("parallel", "arbitrary")),
        name="ffn_ple",
    )(x2, g2, w1, w2, g3, wpg, p2, wpe)


def _block_diag(blocks):
    n, c, d = blocks.shape
    eye = jnp.eye(n, dtype=blocks.dtype)
    return jnp.einsum('ncd,nm->ncmd', blocks, eye).reshape(n * c, n * d)


def _permute_w_in(w_in):
    sizes = (1024, 1024, 1024, 1024, 1024, 384, 256, 64, 512, 512, 1024, 1024, 4096)
    offs = [0]
    for s in sizes:
        offs.append(offs[-1] + s)
    (a_x, a_g, b_q, b_k, b_v, c_q, c_kv, c_kpe, d_q, d_k, d_v, d_g, gates) = [
        w_in[:, offs[n]:offs[n + 1]] for n in range(len(sizes))]
    pad = jnp.zeros((w_in.shape[0], C_WIDTH - 704), w_in.dtype)
    return jnp.concatenate([gates, a_x, a_g, b_q, b_k, b_v, d_v, d_g, d_q, d_k,
                            c_q, c_kv, c_kpe, pad], axis=1).astype(BF16)


def _lru_weights(wa, ba, wx, bx):
    w = MXU_DIM
    nt = D_MODEL // w
    dense = [_block_diag(m) for m in (wa[0], wx[0], wa[1], wx[1])]
    wd = jnp.stack([jnp.concatenate([d[c * w:(c + 1) * w, c * w:(c + 1) * w] for d in dense], axis=1)
                    for c in range(nt)]).astype(BF16)
    bias = (ba[0], bx[0], ba[1], bx[1])
    bd = jnp.stack([jnp.concatenate([b[c * w:(c + 1) * w] for b in bias])[None, :]
                    for c in range(nt)])
    return wd, bd


def _mla_weights(wuq, wukv, gq, gk):
    dh = MLA_NOPE + MLA_ROPE
    zpad = MLA_HEAD_PAD - dh
    wq = wuq.reshape(MLA_Q_LORA, MLA_HEADS, dh)
    wq = jnp.pad(wq, ((0, 0), (0, 0), (0, zpad))).reshape(MLA_Q_LORA, MLA_HEADS * MLA_HEAD_PAD)
    wkv = wukv.reshape(MLA_KV_LORA, MLA_HEADS, MLA_NOPE + MLA_V)
    wk = wkv[:, :, :MLA_NOPE].reshape(MLA_KV_LORA, MLA_HEADS * MLA_NOPE)
    wv = wkv[:, :, MLA_NOPE:].reshape(MLA_KV_LORA, MLA_HEADS * MLA_V)
    gqp = jnp.pad(gq, (0, zpad))[None, :]
    gkp = jnp.pad(gk, (0, zpad))[None, :]
    return wq.astype(BF16), jnp.concatenate([wk, wv], axis=1).astype(BF16), gqp, gkp


def kernel(x, p, norm1_g, w_in, gate_b, conv_w, conv_b, lru_wa, lru_ba, lru_wx, lru_bx, lru_lambda,
           diff_q_g, diff_k_g, diff_lam, diff_sub_g, mla_qa_g, mla_wuq, mla_kva_g, mla_wukv,
           mla_q_g, mla_k_g, ret_gn_g, w_br_a, w_br_b, w_br_c, w_br_d, w_out, norm2_g,
           w_ff1, w_ff2, norm3_g, w_ple_gate, w_ple_proj):
    batch, seq, d_model = x.shape
    depth = w_in.shape[0]
    assert d_model == D_MODEL and seq % (8 * N_SEG) == 0
    m = batch * seq
    tm_big = min(1024, seq)
    tm_small = min(512, seq)
    tq = min(512, seq)
    tk = min(512, seq)

    tabs_b = _rope_cs_tables(seq, DIFF_ROT, DIFF_DH, ROPE_THETA)
    tabs_c = _rope_cs_tables(seq, MLA_ROPE, LANES, ROPE_THETA)
    rp_b = _rope_perm(MXU_DIM, DIFF_ROT, DIFF_DH)
    rp_c = _rope_perm(LANES, MLA_ROPE, LANES)
    tabs_d = _rope_lane_tables(seq, RET_DK, RET_DK, RET_THETA)
    log_g = jnp.log1p(-jnp.exp2(-5.0 - jnp.arange(RET_HEADS, dtype=F32)))
    lg = jnp.broadcast_to(log_g.reshape(RET_HEADS // 2, 2, 1), (RET_HEADS // 2, 2, LANES))
    grp = jnp.arange(MXU_DIM) // DIFF_DH
    gm = ((grp[:, None] == grp[None, :]).astype(F32) / DIFF_DH).astype(BF16)
    ones_row = jnp.ones((1, LANES), F32)

    x2 = x.reshape(m, d_model)
    for i in range(depth):
        proj = _inproj(x2, norm1_g[i][None, :], _permute_w_in(w_in[i]), tm_big, N_PROJ // 3)

        wd, bd = _lru_weights(lru_wa[i], lru_ba[i], lru_wx[i], lru_bx[i])
        y_a = _rglru(proj, conv_w[i], conv_b[i][None, :], wd, bd, lru_lambda[i], batch, seq)

        lam_init = 0.8 - 0.6 * math.exp(-0.3 * i)
        lp = diff_lam[i].astype(F32)
        lam = jnp.exp(jnp.sum(lp[0] * lp[1])) - jnp.exp(jnp.sum(lp[2] * lp[3])) + lam_init
        gq = jnp.tile(diff_q_g[i], MXU_DIM // DIFF_DH)[None, :]
        gk = jnp.tile(diff_k_g[i], MXU_DIM // DIFF_DH)[None, :]
        qb, kb = _diff_prep(proj, gq, gk, tabs_b, gm, rp_b, seq, tm_small, DIFF_DH ** -0.5 * LOG2E)
        attn_b = functools.partial(_attention, v_off=OFF_BV, batch=batch, seq=seq, heads=DIFF_HEADS,
                                   dq=LANES, tq=tq, tk=tk, diff=True, out_scale=1.0 - lam_init)
        y_b = lax.cond(
            _score_bound(diff_q_g[i], diff_k_g[i], DIFF_DH) <= MAX_SCORE_LOG2,
            functools.partial(attn_b, bounded=True), functools.partial(attn_b, bounded=False),
            qb, kb, proj, jnp.full((1, LANES), lam, F32), diff_sub_g[i][None, :])

        wq, wkv, gqp, gkp = _mla_weights(mla_wuq[i], mla_wukv[i], mla_q_g[i], mla_k_g[i])
        qc, kc, vc = _mla_prep(proj, mla_qa_g[i][None, :], wq, mla_kva_g[i][None, :], wkv,
                               gqp, gkp, tabs_c, rp_c, seq, tm_small,
                               (MLA_NOPE + MLA_ROPE) ** -0.5 * LOG2E)
        attn_c = functools.partial(_attention, v_off=0, batch=batch, seq=seq, heads=MLA_HEADS,
                                   dq=MLA_HEAD_PAD, tq=tq, tk=tk, diff=False, out_scale=1.0)
        y_c = lax.cond(
            _score_bound(mla_q_g[i], mla_k_g[i], MLA_NOPE + MLA_ROPE) <= MAX_SCORE_LOG2,
            functools.partial(attn_c, bounded=True), functools.partial(attn_c, bounded=False),
            qc, kc, vc, ones_row, ones_row)

        y_d = _retention(proj, lg, ret_gn_g[i][None, :], tabs_d, batch, seq)

        x2 = _merge(x2, proj, gate_b[i].reshape(1, N_BRANCH * D_MODEL), (y_a, y_b, y_c, y_d),
                    tuple(w[i].astype(BF16) for w in (w_br_a, w_br_b, w_br_c, w_br_d)),
                    w_out[i].astype(BF16), min(256, seq))

        x2 = _ffn(x2, norm2_g[i][None, :], w_ff1[i].astype(BF16), w_ff2[i].astype(BF16),
                  norm3_g[i][None, :], w_ple_gate[i].astype(BF16),
                  p[i].reshape(m, p.shape[-1]), w_ple_proj[i].astype(BF16), tm_big, 1024)
    return x2.reshape(batch, seq, d_model)
```

```python
import functools
import math

import jax
import jax.numpy as jnp
from jax import lax
from jax.experimental import pallas as pl
from jax.experimental.pallas import tpu as pltpu

F32 = jnp.float32
BF16 = jnp.bfloat16

EPS = 1e-6
ROPE_THETA = 500000.0
RET_THETA = 10000.0
LRU_C = 8.0
LOG2E = 1.4426950408889634

LANES = 128
MXU_DIM = 256
VMEM_LIMIT = 56 * 1024 * 1024

D_MODEL = 1024
N_BRANCH = 4
LRU_BLOCKS = 16
CONV_WIDTH = 4
DIFF_HEADS = 8
DIFF_DH = 64
DIFF_ROT = 16
MLA_HEADS = 8
MLA_NOPE = 128
MLA_ROPE = 64
MLA_V = 128
MLA_Q_LORA = 384
MLA_KV_LORA = 256
MLA_HEAD_PAD = 256
RET_HEADS = 8
RET_DK = 64
RET_DV = 128
RET_CHUNK = 256
ATTN_ROW_BLOCK = 128
MAX_SCORE_LOG2 = 60.0
N_SEG = 8
SEG_PAD = 4

OFF_GATE = 0
OFF_AX = 4096
OFF_AG = 5120
OFF_BQ = 6144
OFF_BK = 7168
OFF_BV = 8192
OFF_DV = 9216
OFF_DG = 10240
OFF_DQ = 11264
OFF_DK = 11776
OFF_C = 12288
C_WIDTH = 768
N_PROJ = OFF_C + C_WIDTH


def _cparams(sem):
    return pltpu.CompilerParams(dimension_semantics=sem, vmem_limit_bytes=VMEM_LIMIT)


def _sigmoid(x):
    return 1.0 / (1.0 + jnp.exp2(x * -LOG2E))


def _rms(x, g):
    return x * lax.rsqrt(jnp.mean(x * x, axis=-1, keepdims=True) + EPS) * g


def _dot(a, b):
    return jnp.dot(a, b, preferred_element_type=F32)


def _dot_nt(a, b):
    return lax.dot_general(a, b, (((1,), (1,)), ((), ())), preferred_element_type=F32)


def _dot_tn(a, b):
    return lax.dot_general(a, b, (((0,), (0,)), ((), ())), preferred_element_type=F32)


def _inproj_kernel(x_ref, g_ref, w_ref, o_ref, xn_ref):
    @pl.when(pl.program_id(1) == 0)
    def _():
        xn_ref[...] = _rms(x_ref[...], g_ref[...]).astype(BF16)

    o_ref[...] = _dot(xn_ref[...], w_ref[...]).astype(o_ref.dtype)


def _inproj(x2, g, w, tm, tn):
    m, k = x2.shape
    n = w.shape[1]
    return pl.pallas_call(
        _inproj_kernel,
        grid=(m // tm, n // tn),
        in_specs=[pl.BlockSpec((tm, k), lambda i, j: (i, 0)),
                  pl.BlockSpec((1, k), lambda i, j: (0, 0)),
                  pl.BlockSpec((k, tn), lambda i, j: (0, j))],
        out_specs=pl.BlockSpec((tm, tn), lambda i, j: (i, j)),
        out_shape=jax.ShapeDtypeStruct((m, n), BF16),
        scratch_shapes=[pltpu.VMEM((tm, k), BF16)],
        compiler_params=_cparams(("parallel", "arbitrary")),
        name="inproj",
    )(x2, g, w)


def _gelu_tanh(x):
    c = 0.7978845608028654
    return (0.5 * x) * (1.0 + jnp.tanh(x * (c + (c * 0.044715) * (x * x))))


def _rglru_kernel(x_ref, ga_ref, cw_ref, cb_ref, wd_ref, bd_ref, lam_ref, o_ref,
                  xpad, a0, u0, a1, u1, *, seq):
    w = MXU_DIM
    nslab = w // LANES
    seg = seq // N_SEG
    pitch = seg + SEG_PAD
    left = CONV_WIDTH // 2

    xpad[0:8, :] = jnp.zeros((8, w), F32)
    xpad[seq + 8:seq + 16, :] = jnp.zeros((8, w), F32)
    for s in range(N_SEG):
        xpad[8 + s * seg:8 + (s + 1) * seg, :] = x_ref[s * seg:(s + 1) * seg, :].astype(F32)

    lam = lam_ref[...]
    nlam = -lam
    nls = (-LRU_C * LOG2E) * (jnp.maximum(nlam, 0.0) + jnp.log(1.0 + jnp.exp(-jnp.abs(nlam))))

    a_scr = (a0, a1)
    u_scr = (u0, u1)
    for s in range(N_SEG):
        base = 8 + s * seg - left
        xc = cb_ref[...]
        for t in range(CONV_WIDTH):
            xc = xc + xpad[base + t:base + t + seg, :] * cw_ref[t:t + 1, :]
        z = _dot(xc.astype(BF16), wd_ref[0]) + bd_ref[0]
        for d in range(2):
            r = _sigmoid(z[:, (2 * d) * w:(2 * d + 1) * w])
            gi = _sigmoid(z[:, (2 * d + 1) * w:(2 * d + 2) * w])
            a = jnp.exp2(nls[d:d + 1, :] * r)
            v = 1.0 - a * a
            u = (v * lax.rsqrt(jnp.maximum(v, 1e-30))) * (gi * xc)
            for sl in range(nslab):
                a_scr[d][sl, s * pitch:s * pitch + seg, :] = a[:, sl * LANES:(sl + 1) * LANES]
                u_scr[d][sl, s * pitch:s * pitch + seg, :] = u[:, sl * LANES:(sl + 1) * LANES]

    def rows(t):
        return pl.ds(t, N_SEG, stride=pitch)

    def scan_body(t, carry):
        new = []
        tb = seg - 1 - t
        for sl in range(nslab):
            hf, pf, hb, pb = carry[sl]
            af = a0[sl, rows(t), :]
            hf = af * hf + u0[sl, rows(t), :]
            pf = pf * af
            u0[sl, rows(t), :] = hf
            a0[sl, rows(t), :] = pf
            ab = a1[sl, rows(tb), :]
            hb = ab * hb + u1[sl, rows(tb), :]
            pb = pb * ab
            u1[sl, rows(tb), :] = hb
            a1[sl, rows(tb), :] = pb
            new.append((hf, pf, hb, pb))
        return tuple(new)

    zero = jnp.zeros((N_SEG, LANES), F32)
    one = jnp.ones((N_SEG, LANES), F32)
    fin = lax.fori_loop(0, seg, scan_body, tuple((zero, one, zero, one) for _ in range(nslab)),
                        unroll=4)

    cf, cb = [], []
    for sl in range(nslab):
        hf, pf, hb, pb = fin[sl]
        c = jnp.zeros((1, LANES), F32)
        cfr = [c]
        for s in range(1, N_SEG):
            c = pf[s - 1:s, :] * c + hf[s - 1:s, :]
            cfr.append(c)
        cf.append(jnp.concatenate(cfr, axis=0))
        c = jnp.zeros((1, LANES), F32)
        cbr = [c]
        for s in range(N_SEG - 2, -1, -1):
            c = pb[s + 1:s + 2, :] * c + hb[s + 1:s + 2, :]
            cbr.append(c)
        cb.append(jnp.concatenate(cbr[::-1], axis=0))

    def fix_body(t, carry):
        for sl in range(nslab):
            hf = u0[sl, rows(t), :] + a0[sl, rows(t), :] * cf[sl]
            hb = u1[sl, rows(t), :] + a1[sl, rows(t), :] * cb[sl]
            u0[sl, rows(t), :] = hf + hb
        return carry

    lax.fori_loop(0, seg, fix_body, 0, unroll=4)

    for s in range(N_SEG):
        g = _gelu_tanh(ga_ref[s * seg:(s + 1) * seg, :].astype(F32))
        for sl in range(nslab):
            h = u0[sl, s * pitch:s * pitch + seg, :]
            o_ref[s * seg:(s + 1) * seg, sl * LANES:(sl + 1) * LANES] = (
                h * g[:, sl * LANES:(sl + 1) * LANES]).astype(o_ref.dtype)


def _rglru(proj, cw, cb, wd, bd, lam, batch, seq):
    w = MXU_DIM
    nt = D_MODEL // w
    nslab = w // LANES
    rows = N_SEG * (seq // N_SEG + SEG_PAD)
    scan_scratch = pltpu.VMEM((nslab, rows, LANES), F32)
    return pl.pallas_call(
        functools.partial(_rglru_kernel, seq=seq),
        grid=(batch, nt),
        in_specs=[pl.BlockSpec((seq, w), lambda b, c: (b, OFF_AX // w + c)),
                  pl.BlockSpec((seq, w), lambda b, c: (b, OFF_AG // w + c)),
                  pl.BlockSpec((CONV_WIDTH, w), lambda b, c: (0, c)),
                  pl.BlockSpec((1, w), lambda b, c: (0, c)),
                  pl.BlockSpec((1, w, 4 * w), lambda b, c: (c, 0, 0)),
                  pl.BlockSpec((1, 1, 4 * w), lambda b, c: (c, 0, 0)),
                  pl.BlockSpec((2, w), lambda b, c: (0, c))],
        out_specs=pl.BlockSpec((seq, w), lambda b, c: (b, c)),
        out_shape=jax.ShapeDtypeStruct((batch * seq, D_MODEL), BF16),
        scratch_shapes=[pltpu.VMEM((seq + 16, w), F32),
                        scan_scratch, scan_scratch, scan_scratch, scan_scratch],
        compiler_params=_cparams(("parallel", "parallel")),
        name="rglru",
    )(proj, proj, cw, cb, wd, bd, lam)


def _rope_mxu(x, perm, c, s):
    return x * c + _dot(x.astype(BF16), perm) * s


def _rope_perm(width, rot, period):
    half = rot // 2
    lane = jnp.arange(width)
    j = lane % period
    src = jnp.where(j < half, lane + half, lane - half)
    sign = jnp.where(j < half, -1.0, 1.0) * (j < rot)
    return ((lane[:, None] == src[None, :]) * sign[None, :]).astype(BF16)


def _rope_cs_tables(seq, rot, period, theta):
    pos = jnp.arange(seq, dtype=F32)
    inv_freq = theta ** (-jnp.arange(0, rot, 2, dtype=F32) / rot)
    ang = pos[:, None] * inv_freq[None, :]
    cos, sin = jnp.cos(ang), jnp.sin(ang)
    pad = period - rot
    c = jnp.concatenate([cos, cos, jnp.ones((seq, pad), F32)], axis=1)
    s = jnp.concatenate([sin, sin, jnp.zeros((seq, pad), F32)], axis=1)
    rep = LANES // period
    return jnp.tile(c, (1, rep)), jnp.tile(s, (1, rep))


def _diff_prep_kernel(q_ref, k_ref, gq_ref, gk_ref, c_ref, s_ref, gm_ref, rp_ref,
                      qo_ref, ko_ref, *, qscale):
    c = jnp.concatenate([c_ref[...]] * (MXU_DIM // LANES), axis=1)
    sn = jnp.concatenate([s_ref[...]] * (MXU_DIM // LANES), axis=1)
    gm = gm_ref[...]
    rp = rp_ref[...]
    for src, g_ref, is_key in ((q_ref, gq_ref, False), (k_ref, gk_ref, True)):
        for blk in range(D_MODEL // MXU_DIM):
            x = src[:, blk * MXU_DIM:(blk + 1) * MXU_DIM].astype(F32)
            ms = _dot((x * x).astype(BF16), gm)
            xr = _rope_mxu(x * lax.rsqrt(ms + EPS) * g_ref[...], rp, c, sn)
            for h in range(MXU_DIM // LANES):
                xb = xr[:, h * LANES:(h + 1) * LANES]
                lo = blk * MXU_DIM + h * LANES
                if is_key:
                    ko_ref[lo:lo + LANES, :] = xb.T.astype(ko_ref.dtype)
                else:
                    qo_ref[:, lo:lo + LANES] = (xb * qscale).astype(qo_ref.dtype)


def _diff_prep(proj, gq, gk, tabs, gm, rp, seq, tm, qscale):
    m = proj.shape[0]
    nrow = seq // tm
    tab_spec = pl.BlockSpec((tm, LANES), lambda i: (i % nrow, 0))
    return pl.pallas_call(
        functools.partial(_diff_prep_kernel, qscale=qscale),
        grid=(m // tm,),
        in_specs=[pl.BlockSpec((tm, D_MODEL), lambda i: (i, OFF_BQ // D_MODEL)),
                  pl.BlockSpec((tm, D_MODEL), lambda i: (i, OFF_BK // D_MODEL)),
                  pl.BlockSpec((1, MXU_DIM), lambda i: (0, 0)),
                  pl.BlockSpec((1, MXU_DIM), lambda i: (0, 0)),
                  tab_spec, tab_spec,
                  pl.BlockSpec((MXU_DIM, MXU_DIM), lambda i: (0, 0)),
                  pl.BlockSpec((MXU_DIM, MXU_DIM), lambda i: (0, 0))],
        out_specs=[pl.BlockSpec((tm, D_MODEL), lambda i: (i, 0)),
                   pl.BlockSpec((None, D_MODEL, tm), lambda i: (i // nrow, 0, i % nrow))],
        out_shape=[jax.ShapeDtypeStruct((m, D_MODEL), BF16),
                   jax.ShapeDtypeStruct((m // seq, D_MODEL, seq), BF16)],
        compiler_params=_cparams(("parallel",)),
        name="diff_prep",
    )(proj, proj, gq, gk, *tabs, gm, rp)


def _mla_prep_kernel(c_ref, qag_ref, wuq_ref, kvag_ref, wukv_ref, gq_ref, gk_ref,
                     c_tab, s_tab, rp_ref, qo_ref, ko_ref, vo_ref, *, qscale):
    dh = MLA_NOPE + MLA_ROPE
    ct, st, rp = c_tab[...], s_tab[...], rp_ref[...]

    def sumsq(x):
        return _dot((x * x).astype(BF16), jnp.ones((x.shape[1], LANES), BF16))

    def rms_wide(x, g):
        inv = lax.rsqrt(sumsq(x) * (1.0 / x.shape[1]) + EPS)
        return x * jnp.concatenate([inv] * (x.shape[1] // LANES), axis=1) * g

    cq = c_ref[:, 0:MLA_Q_LORA].astype(F32)
    ckv = c_ref[:, MLA_Q_LORA:MLA_Q_LORA + MLA_KV_LORA].astype(F32)
    kpe = c_ref[:, MLA_Q_LORA + MLA_KV_LORA:C_WIDTH].astype(F32)
    q = _dot(rms_wide(cq, qag_ref[...]).astype(BF16), wuq_ref[...])
    kv = _dot(rms_wide(ckv, kvag_ref[...]).astype(BF16), wukv_ref[...])
    gq_n, gq_r = gq_ref[:, 0:LANES], gq_ref[:, LANES:2 * LANES]
    gk_n, gk_r = gk_ref[:, 0:LANES], gk_ref[:, LANES:2 * LANES]
    kpe_ss = sumsq(kpe)
    for h in range(MLA_HEADS):
        lo = h * MLA_HEAD_PAD
        qn, qr = q[:, lo:lo + LANES], q[:, lo + LANES:lo + 2 * LANES]
        qh = q[:, lo:lo + 2 * LANES]
        inv = lax.rsqrt(jnp.sum(qh * qh, axis=-1, keepdims=True) * (1.0 / dh) + EPS) * qscale
        qo_ref[:, lo:lo + LANES] = (qn * inv * gq_n).astype(qo_ref.dtype)
        qo_ref[:, lo + LANES:lo + 2 * LANES] = _rope_mxu(
            qr * inv * gq_r, rp, ct, st).astype(qo_ref.dtype)
        kn = kv[:, h * LANES:(h + 1) * LANES]
        inv = lax.rsqrt((jnp.sum(kn * kn, axis=-1, keepdims=True) + kpe_ss) * (1.0 / dh) + EPS)
        ko_ref[lo:lo + LANES, :] = (kn * inv * gk_n).T.astype(ko_ref.dtype)
        ko_ref[lo + LANES:lo + 2 * LANES, :] = _rope_mxu(
            kpe * inv * gk_r, rp, ct, st).T.astype(ko_ref.dtype)
    vo_ref[...] = kv[:, MLA_HEADS * LANES:].astype(vo_ref.dtype)


def _mla_prep(proj, qag, wuq, kvag, wukv, gq, gk, tabs, rp, seq, tm, qscale):
    m = proj.shape[0]
    nrow = seq // tm
    hp = MLA_HEADS * MLA_HEAD_PAD
    tab_spec = pl.BlockSpec((tm, LANES), lambda i: (i % nrow, 0))
    const = lambda shape: pl.BlockSpec(shape, lambda i: (0, 0))
    return pl.pallas_call(
        functools.partial(_mla_prep_kernel, qscale=qscale),
        grid=(m // tm,),
        in_specs=[pl.BlockSpec((tm, C_WIDTH), lambda i: (i, OFF_C // C_WIDTH)),
                  const((1, MLA_Q_LORA)), const((MLA_Q_LORA, hp)),
                  const((1, MLA_KV_LORA)), const((MLA_KV_LORA, 2 * MLA_HEADS * LANES)),
                  const((1, MLA_HEAD_PAD)), const((1, MLA_HEAD_PAD)),
                  tab_spec, tab_spec, const((LANES, LANES))],
        out_specs=[pl.BlockSpec((tm, hp), lambda i: (i, 0)),
                   pl.BlockSpec((None, hp, tm), lambda i: (i // nrow, 0, i % nrow)),
                   pl.BlockSpec((tm, MLA_HEADS * MLA_V), lambda i: (i, 0))],
        out_shape=[jax.ShapeDtypeStruct((m, hp), BF16),
                   jax.ShapeDtypeStruct((m // seq, hp, seq), BF16),
                   jax.ShapeDtypeStruct((m, MLA_HEADS * MLA_V), BF16)],
        compiler_params=_cparams(("parallel",)),
        name="mla_prep",
    )(proj, qag, wuq, kvag, wukv, gq, gk, *tabs, rp)


def _attn_kernel(q_ref, k_ref, v_ref, lam_ref, sg_ref, o_ref, vext, m_scr, acc_scr,
                 *, seq, tk, diff, out_scale):
    tq = q_ref.shape[0]

    @pl.when(pl.program_id(2) == 0)
    def _():
        vext[:, 0:LANES] = v_ref[...]
        vext[:, LANES:2 * LANES] = jnp.ones((seq, LANES), BF16)

    q = q_ref[...]
    if diff:
        lane = lax.broadcasted_iota(jnp.int32, q.shape, 1)
        zero = jnp.zeros_like(q)
        q = jnp.concatenate([jnp.where(lane < DIFF_DH, q, zero),
                             jnp.where(lane >= DIFF_DH, q, zero)], axis=0)
    rows = q.shape[0]
    m_scr[...] = jnp.full((rows, LANES), -1e30, F32)
    acc_scr[...] = jnp.zeros((rows, 2 * LANES), F32)

    def body(j, carry):
        off = pl.multiple_of(j * tk, tk)
        s = _dot(q, k_ref[:, pl.ds(off, tk)])
        m_prev = m_scr[...]
        m_new = jnp.maximum(m_prev, jnp.max(s, axis=1, keepdims=True))
        alpha = jnp.exp2(m_prev - m_new)
        p = jnp.exp2(s - m_new[:, 0:1]).astype(BF16)
        acc_scr[...] = (acc_scr[...] * jnp.concatenate([alpha, alpha], axis=1)
                        + _dot(p, vext[pl.ds(off, tk), :]))
        m_scr[...] = m_new
        return carry

    lax.fori_loop(0, seq // tk, body, 0)

    acc = acc_scr[...]
    o = acc[:, 0:LANES] / acc[:, LANES:2 * LANES]
    if diff:
        o = o[0:tq, :] - lam_ref[...] * o[tq:2 * tq, :]
        o = _rms(o, sg_ref[...]) * out_scale
    o_ref[...] = o.astype(o_ref.dtype)


def _attn_bounded_kernel(q_ref, k_ref, v_ref, lam_ref, sg_ref, o_ref, vext,
                         *, seq, tk, diff, out_scale):
    tq = q_ref.shape[0]

    @pl.when(pl.program_id(2) == 0)
    def _():
        vext[:, 0:LANES] = v_ref[...]
        vext[:, LANES:2 * LANES] = jnp.ones((seq, LANES), BF16)

    q = q_ref[...]
    if diff:
        lane = lax.broadcasted_iota(jnp.int32, q.shape, 1)
        zero = jnp.zeros_like(q)
        q = jnp.concatenate([jnp.where(lane < DIFF_DH, q, zero),
                             jnp.where(lane >= DIFF_DH, q, zero)], axis=0)
    rb = ATTN_ROW_BLOCK

    def softmax_rows(lo):
        acc = None
        for j in range(seq // tk):
            p = jnp.exp2(_dot(q[lo:lo + rb, :], k_ref[:, j * tk:(j + 1) * tk])).astype(BF16)
            pv = _dot(p, vext[j * tk:(j + 1) * tk, :])
            acc = pv if acc is None else acc + pv
        return acc[:, 0:LANES] / acc[:, LANES:2 * LANES]

    for r in range(tq // rb):
        o = softmax_rows(r * rb)
        if diff:
            o = o - lam_ref[...] * softmax_rows(tq + r * rb)
            o = _rms(o, sg_ref[...]) * out_scale
        o_ref[r * rb:(r + 1) * rb, :] = o.astype(o_ref.dtype)


def _score_bound(gq, gk, dh):
    return 1.02 * LOG2E * math.sqrt(dh) * jnp.max(jnp.abs(gq)) * jnp.max(jnp.abs(gk))


def _attention(q, k, v, lam, sg, *, v_off, batch, seq, heads, dq, tq, tk, diff, out_scale, bounded):
    nq = seq // tq
    rows = 2 * tq if diff else tq
    vblk = v_off // LANES
    body = _attn_bounded_kernel if bounded else _attn_kernel
    return pl.pallas_call(
        functools.partial(body, seq=seq, tk=tk, diff=diff, out_scale=out_scale),
        grid=(batch, heads, nq),
        in_specs=[pl.BlockSpec((tq, dq), lambda b, h, i: (b * nq + i, h)),
                  pl.BlockSpec((None, dq, seq), lambda b, h, i: (b, h, 0)),
                  pl.BlockSpec((seq, LANES), lambda b, h, i: (b, vblk + h)),
                  pl.BlockSpec((1, LANES), lambda b, h, i: (0, 0)),
                  pl.BlockSpec((1, LANES), lambda b, h, i: (0, 0))],
        out_specs=pl.BlockSpec((tq, LANES), lambda b, h, i: (b * nq + i, h)),
        out_shape=jax.ShapeDtypeStruct((batch * seq, heads * LANES), BF16),
        scratch_shapes=[pltpu.VMEM((seq, 2 * LANES), BF16)] + ([] if bounded else [
            pltpu.VMEM((rows, LANES), F32), pltpu.VMEM((rows, 2 * LANES), F32)]),
        compiler_params=_cparams(("parallel", "parallel", "arbitrary")),
        name=("diff_attn" if diff else "mla_attn") + ("_bounded" if bounded else "_online"),
    )(q, k, v, lam, sg)


def _retention_kernel(q_ref, k_ref, v_ref, g_ref, lg_ref, gn_ref, c_tab, s_tab, rp_ref,
                      o_ref, qs, kts, o_fwd, o_bwd, state, *, seq):
    ch = RET_CHUNK if seq % RET_CHUNK == 0 else seq
    nc = seq // ch
    ct, st, rp = c_tab[...], s_tab[...], rp_ref[...]
    qs[...] = _rope_mxu(q_ref[...].astype(F32), rp, ct, st)
    kt = (_rope_mxu(k_ref[...].astype(F32), rp, ct, st) * (RET_DK ** -0.5)).T
    row = lax.broadcasted_iota(jnp.int32, kt.shape, 0)
    kts[0] = jnp.where(row < RET_DK, kt, 0.0)
    kts[1] = jnp.where(row >= RET_DK, kt, 0.0)
    state[...] = jnp.zeros(state.shape, F32)

    ri = lax.broadcasted_iota(jnp.int32, (ch, ch), 0).astype(F32)
    ci = lax.broadcasted_iota(jnp.int32, (ch, ch), 1).astype(F32)
    qidx = lax.broadcasted_iota(jnp.int32, (ch, 1), 0).astype(F32)
    kidx = lax.broadcasted_iota(jnp.int32, (1, ch), 1).astype(F32)
    dec = []
    for hh in range(2):
        lg = lg_ref[hh:hh + 1, 0:1]
        dec.append(dict(
            intra=jnp.exp(lg * jnp.abs(ri - ci)),
            tail=jnp.exp(lg * (ch - 1.0 - kidx)),
            head=jnp.exp(lg * kidx),
            qf=jnp.exp(lg * (qidx + 1.0)),
            qb=jnp.exp(lg * (ch - qidx)),
            chunk=jnp.exp(lg * float(ch))))

    def sweep(t, carry):
        of = pl.multiple_of(t * ch, ch)
        ob = pl.multiple_of((nc - 1 - t) * ch, ch)
        qf = qs[pl.ds(of, ch), :]
        qb = qs[pl.ds(ob, ch), :]
        for hh in range(2):
            d = dec[hh]
            vlo = hh * RET_DV
            ktf = kts[hh, :, pl.ds(of, ch)]
            vf = v_ref[pl.ds(of, ch), vlo:vlo + RET_DV]
            sc = _dot(qf.astype(BF16), ktf.astype(BF16)) * d["intra"]
            past = state[2 * hh]
            o_fwd[hh, pl.ds(of, ch), :] = (_dot(sc.astype(BF16), vf)
                                           + _dot((qf * d["qf"]).astype(BF16), past.astype(BF16)))
            state[2 * hh] = past * d["chunk"] + _dot((ktf * d["tail"]).astype(BF16), vf)
            ktb = kts[hh, :, pl.ds(ob, ch)]
            vb = v_ref[pl.ds(ob, ch), vlo:vlo + RET_DV]
            fut = state[2 * hh + 1]
            o_bwd[hh, pl.ds(ob, ch), :] = _dot((qb * d["qb"]).astype(BF16), fut.astype(BF16))
            state[2 * hh + 1] = fut * d["chunk"] + _dot((ktb * d["head"]).astype(BF16), vb)
        return carry

    lax.fori_loop(0, nc, sweep, 0, unroll=2 if nc % 2 == 0 else 1)

    lane_mean = jnp.full((RET_DV, RET_DV), 1.0 / RET_DV, BF16)

    def fin(n, carry):
        off = pl.multiple_of(n * ch, ch)
        for hh in range(2):
            vlo = hh * RET_DV
            o = o_fwd[hh, pl.ds(off, ch), :] + o_bwd[hh, pl.ds(off, ch), :]
            oc = o - _dot(o.astype(BF16), lane_mean)
            var = _dot((oc * oc).astype(BF16), lane_mean)
            y = oc * lax.rsqrt(var + EPS) * gn_ref[...]
            g = g_ref[pl.ds(off, ch), vlo:vlo + RET_DV].astype(F32)
            o_ref[pl.ds(off, ch), vlo:vlo + RET_DV] = (g * _sigmoid(g) * y).astype(o_ref.dtype)
        return carry

    lax.fori_loop(0, nc, fin, 0, unroll=2 if nc % 2 == 0 else 1)


def _retention(proj, lg, gn, tabs, rp, batch, seq):
    npair = RET_HEADS // 2
    pw = 2 * RET_DV
    tab_spec = pl.BlockSpec((seq, LANES), lambda b, j: (0, 0))
    return pl.pallas_call(
        functools.partial(_retention_kernel, seq=seq),
        grid=(batch, npair),
        in_specs=[pl.BlockSpec((seq, LANES), lambda b, j: (b, OFF_DQ // LANES + j)),
                  pl.BlockSpec((seq, LANES), lambda b, j: (b, OFF_DK // LANES + j)),
                  pl.BlockSpec((seq, pw), lambda b, j: (b, OFF_DV // pw + j)),
                  pl.BlockSpec((seq, pw), lambda b, j: (b, OFF_DG // pw + j)),
                  pl.BlockSpec((None, 2, LANES), lambda b, j: (j, 0, 0)),
                  pl.BlockSpec((1, RET_DV), lambda b, j: (0, 0)),
                  tab_spec, tab_spec, pl.BlockSpec((LANES, LANES), lambda b, j: (0, 0))],
        out_specs=pl.BlockSpec((seq, pw), lambda b, j: (b, j)),
        out_shape=jax.ShapeDtypeStruct((batch * seq, RET_HEADS * RET_DV), BF16),
        scratch_shapes=[pltpu.VMEM((seq, LANES), F32),
                        pltpu.VMEM((2, LANES, seq), F32),
                        pltpu.VMEM((2, seq, RET_DV), F32),
                        pltpu.VMEM((2, seq, RET_DV), F32),
                        pltpu.VMEM((4, LANES, RET_DV), F32)],
        compiler_params=_cparams(("parallel", "parallel")),
        name="retention",
    )(proj, proj, proj, proj, lg, gn, *tabs, rp)


def _merge_kernel(x_ref, gl_ref, gb_ref, ya_ref, yb_ref, yc_ref, yd_ref,
                  wa_ref, wb_ref, wc_ref, wd_ref, wo_ref, o_ref):
    merged = None
    for n, (y_ref, w_ref) in enumerate(((ya_ref, wa_ref), (yb_ref, wb_ref),
                                        (yc_ref, wc_ref), (yd_ref, wd_ref))):
        lo = n * D_MODEL
        gate = _sigmoid(gl_ref[:, lo:lo + D_MODEL].astype(F32) + gb_ref[:, lo:lo + D_MODEL])
        term = gate * _dot(y_ref[...], w_ref[...])
        merged = term if merged is None else merged + term
    o_ref[...] = x_ref[...] + _dot(merged.astype(BF16), wo_ref[...])


def _merge(x2, proj, gb, ys, ws, wo, tm):
    m = x2.shape[0]
    row = lambda width: pl.BlockSpec((tm, width), lambda i: (i, 0))
    wspec = pl.BlockSpec((D_MODEL, D_MODEL), lambda i: (0, 0))
    return pl.pallas_call(
        _merge_kernel,
        grid=(m // tm,),
        in_specs=[row(D_MODEL), row(N_BRANCH * D_MODEL),
                  pl.BlockSpec((1, N_BRANCH * D_MODEL), lambda i: (0, 0)),
                  row(D_MODEL), row(D_MODEL), row(D_MODEL), row(D_MODEL),
                  wspec, wspec, wspec, wspec, wspec],
        out_specs=row(D_MODEL),
        out_shape=jax.ShapeDtypeStruct((m, D_MODEL), F32),
        compiler_params=_cparams(("parallel",)),
        name="merge",
    )(x2, proj, gb, *ys, *ws, wo)


def _ffn_kernel(x_ref, g2_ref, w1_ref, w2_ref, g3_ref, wpg_ref, p_ref, wpe_ref, o_ref,
                xn_ref, acc_ref):
    j = pl.program_id(1)

    @pl.when(j == 0)
    def _():
        xn_ref[...] = _rms(x_ref[...], g2_ref[...]).astype(BF16)
        acc_ref[...] = jnp.zeros_like(acc_ref)

    h = jnp.maximum(_dot(xn_ref[...], w1_ref[...]), 0.0)
    acc_ref[...] += _dot((h * h).astype(BF16), w2_ref[...])

    @pl.when(j == pl.num_programs(1) - 1)
    def _():
        x = x_ref[...] + acc_ref[...]
        gate = _sigmoid(_dot(_rms(x, g3_ref[...]).astype(BF16), wpg_ref[...]))
        o_ref[...] = x + gate * _dot(p_ref[...].astype(BF16), wpe_ref[...])


def _ffn(x2, g2, w1, w2, g3, wpg, p2, layer, wpe, tm, tf):
    m = x2.shape[0]
    dff = w1.shape[1]
    pdim = p2.shape[1]
    p_row0 = layer * (m // tm)
    return pl.pallas_call(
        _ffn_kernel,
        grid=(m // tm, dff // tf),
        in_specs=[pl.BlockSpec((tm, D_MODEL), lambda i, j: (i, 0)),
                  pl.BlockSpec((1, D_MODEL), lambda i, j: (0, 0)),
                  pl.BlockSpec((D_MODEL, tf), lambda i, j: (0, j)),
                  pl.BlockSpec((tf, D_MODEL), lambda i, j: (j, 0)),
                  pl.BlockSpec((1, D_MODEL), lambda i, j: (0, 0)),
                  pl.BlockSpec((D_MODEL, D_MODEL), lambda i, j: (0, 0)),
                  pl.BlockSpec((tm, pdim), lambda i, j: (p_row0 + i, 0)),
                  pl.BlockSpec((pdim, D_MODEL), lambda i, j: (0, 0))],
        out_specs=pl.BlockSpec((tm, D_MODEL), lambda i, j: (i, 0)),
        out_shape=jax.ShapeDtypeStruct((m, D_MODEL), F32),
        scratch_shapes=[pltpu.VMEM((tm, D_MODEL), BF16), pltpu.VMEM((tm, D_MODEL), F32)],
        compiler_params=_cparams(("parallel", "arbitrary")),
        name="ffn_ple",
    )(x2, g2, w1, w2, g3, wpg, p2, wpe)


def _block_diag(blocks):
    n, c, d = blocks.shape
    eye = jnp.eye(n, dtype=blocks.dtype)
    return jnp.einsum('ncd,nm->ncmd', blocks, eye).reshape(n * c, n * d)


def _permute_w_in(w_in):
    sizes = (1024, 1024, 1024, 1024, 1024, 384, 256, 64, 512, 512, 1024, 1024, 4096)
    offs = [0]
    for s in sizes:
        offs.append(offs[-1] + s)
    (a_x, a_g, b_q, b_k, b_v, c_q, c_kv, c_kpe, d_q, d_k, d_v, d_g, gates) = [
        w_in[:, offs[n]:offs[n + 1]] for n in range(len(sizes))]
    pad = jnp.zeros((w_in.shape[0], C_WIDTH - 704), w_in.dtype)
    return jnp.concatenate([gates, a_x, a_g, b_q, b_k, b_v, d_v, d_g, d_q, d_k,
                            c_q, c_kv, c_kpe, pad], axis=1).astype(BF16)


def _lru_weights(wa, ba, wx, bx):
    w = MXU_DIM
    nt = D_MODEL // w
    dense = [_block_diag(m) for m in (wa[0], wx[0], wa[1], wx[1])]
    wd = jnp.stack([jnp.concatenate([d[c * w:(c + 1) * w, c * w:(c + 1) * w] for d in dense], axis=1)
                    for c in range(nt)]).astype(BF16)
    bias = (ba[0], bx[0], ba[1], bx[1])
    bd = jnp.stack([jnp.concatenate([b[c * w:(c + 1) * w] for b in bias])[None, :]
                    for c in range(nt)])
    return wd, bd


def _mla_weights(wuq, wukv, gq, gk):
    dh = MLA_NOPE + MLA_ROPE
    zpad = MLA_HEAD_PAD - dh
    wq = wuq.reshape(MLA_Q_LORA, MLA_HEADS, dh)
    wq = jnp.pad(wq, ((0, 0), (0, 0), (0, zpad))).reshape(MLA_Q_LORA, MLA_HEADS * MLA_HEAD_PAD)
    wkv = wukv.reshape(MLA_KV_LORA, MLA_HEADS, MLA_NOPE + MLA_V)
    wk = wkv[:, :, :MLA_NOPE].reshape(MLA_KV_LORA, MLA_HEADS * MLA_NOPE)
    wv = wkv[:, :, MLA_NOPE:].reshape(MLA_KV_LORA, MLA_HEADS * MLA_V)
    gqp = jnp.pad(gq, (0, zpad))[None, :]
    gkp = jnp.pad(gk, (0, zpad))[None, :]
    return wq.astype(BF16), jnp.concatenate([wk, wv], axis=1).astype(BF16), gqp, gkp


def kernel(x, p, norm1_g, w_in, gate_b, conv_w, conv_b, lru_wa, lru_ba, lru_wx, lru_bx, lru_lambda,
           diff_q_g, diff_k_g, diff_lam, diff_sub_g, mla_qa_g, mla_wuq, mla_kva_g, mla_wukv,
           mla_q_g, mla_k_g, ret_gn_g, w_br_a, w_br_b, w_br_c, w_br_d, w_out, norm2_g,
           w_ff1, w_ff2, norm3_g, w_ple_gate, w_ple_proj):
    batch, seq, d_model = x.shape
    depth = w_in.shape[0]
    assert d_model == D_MODEL and seq % (8 * N_SEG) == 0
    m = batch * seq
    tm_big = min(1024, seq)
    tm_small = min(512, seq)
    tq = min(1024, seq)
    tk = min(512, seq)

    tabs_b = _rope_cs_tables(seq, DIFF_ROT, DIFF_DH, ROPE_THETA)
    tabs_c = _rope_cs_tables(seq, MLA_ROPE, LANES, ROPE_THETA)
    rp_b = _rope_perm(MXU_DIM, DIFF_ROT, DIFF_DH)
    rp_c = _rope_perm(LANES, MLA_ROPE, LANES)
    tabs_d = _rope_cs_tables(seq, RET_DK, RET_DK, RET_THETA)
    rp_d = _rope_perm(LANES, RET_DK, RET_DK)
    log_g = jnp.log1p(-jnp.exp2(-5.0 - jnp.arange(RET_HEADS, dtype=F32)))
    lg = jnp.broadcast_to(log_g.reshape(RET_HEADS // 2, 2, 1), (RET_HEADS // 2, 2, LANES))
    grp = jnp.arange(MXU_DIM) // DIFF_DH
    gm = ((grp[:, None] == grp[None, :]).astype(F32) / DIFF_DH).astype(BF16)
    ones_row = jnp.ones((1, LANES), F32)

    x2 = x.reshape(m, d_model)
    for i in range(depth):
        proj = _inproj(x2, norm1_g[i][None, :], _permute_w_in(w_in[i]), tm_big, N_PROJ // 3)

        wd, bd = _lru_weights(lru_wa[i], lru_ba[i], lru_wx[i], lru_bx[i])
        y_a = _rglru(proj, conv_w[i], conv_b[i][None, :], wd, bd, lru_lambda[i], batch, seq)

        lam_init = 0.8 - 0.6 * math.exp(-0.3 * i)
        lp = diff_lam[i].astype(F32)
        lam = jnp.exp(jnp.sum(lp[0] * lp[1])) - jnp.exp(jnp.sum(lp[2] * lp[3])) + lam_init
        gq = jnp.tile(diff_q_g[i], MXU_DIM // DIFF_DH)[None, :]
        gk = jnp.tile(diff_k_g[i], MXU_DIM // DIFF_DH)[None, :]
        qb, kb = _diff_prep(proj, gq, gk, tabs_b, gm, rp_b, seq, tm_small, DIFF_DH ** -0.5 * LOG2E)
        attn_b = functools.partial(_attention, v_off=OFF_BV, batch=batch, seq=seq, heads=DIFF_HEADS,
                                   dq=LANES, tq=tq, tk=tk, diff=True, out_scale=1.0 - lam_init)
        y_b = lax.cond(
            _score_bound(diff_q_g[i], diff_k_g[i], DIFF_DH) <= MAX_SCORE_LOG2,
            functools.partial(attn_b, bounded=True), functools.partial(attn_b, bounded=False),
            qb, kb, proj, jnp.full((1, LANES), lam, F32), diff_sub_g[i][None, :])

        wq, wkv, gqp, gkp = _mla_weights(mla_wuq[i], mla_wukv[i], mla_q_g[i], mla_k_g[i])
        qc, kc, vc = _mla_prep(proj, mla_qa_g[i][None, :], wq, mla_kva_g[i][None, :], wkv,
                               gqp, gkp, tabs_c, rp_c, seq, tm_small,
                               (MLA_NOPE + MLA_ROPE) ** -0.5 * LOG2E)
        attn_c = functools.partial(_attention, v_off=0, batch=batch, seq=seq, heads=MLA_HEADS,
                                   dq=MLA_HEAD_PAD, tq=tq, tk=tk, diff=False, out_scale=1.0)
        y_c = lax.cond(
            _score_bound(mla_q_g[i], mla_k_g[i], MLA_NOPE + MLA_ROPE) <= MAX_SCORE_LOG2,
            functools.partial(attn_c, bounded=True), functools.partial(attn_c, bounded=False),
            qc, kc, vc, ones_row, ones_row)

        y_d = _retention(proj, lg, ret_gn_g[i][None, :], tabs_d, rp_d, batch, seq)

        x2 = _merge(x2, proj, gate_b[i].reshape(1, N_BRANCH * D_MODEL), (y_a, y_b, y_c, y_d),
                    tuple(w[i].astype(BF16) for w in (w_br_a, w_br_b, w_br_c, w_br_d)),
                    w_out[i].astype(BF16), min(256, seq))

        x2 = _ffn(x2, norm2_g[i][None, :], w_ff1[i].astype(BF16), w_ff2[i].astype(BF16),
                  norm3_g[i][None, :], w_ple_gate[i].astype(BF16),
                  p.reshape(depth * m, p.shape[-1]), i, w_ple_proj[i].astype(BF16), tm_big, 1024)
    return x2.reshape(batch, seq, d_model)
```

```python
import functools
import math

import jax
import jax.numpy as jnp
from jax import lax
from jax.experimental import pallas as pl
from jax.experimental.pallas import tpu as pltpu

F32 = jnp.float32
BF16 = jnp.bfloat16

EPS = 1e-6
ROPE_THETA = 500000.0
RET_THETA = 10000.0
LRU_C = 8.0
LOG2E = 1.4426950408889634

LANES = 128
MXU_DIM = 256
VMEM_LIMIT = 56 * 1024 * 1024

D_MODEL = 1024
N_BRANCH = 4
LRU_BLOCKS = 16
CONV_WIDTH = 4
DIFF_HEADS = 8
DIFF_DH = 64
DIFF_ROT = 16
MLA_HEADS = 8
MLA_NOPE = 128
MLA_ROPE = 64
MLA_V = 128
MLA_Q_LORA = 384
MLA_KV_LORA = 256
MLA_HEAD_PAD = 256
RET_HEADS = 8
RET_DK = 64
RET_DV = 128
RET_CHUNK = 256
ATTN_ROW_BLOCK = 128
NORM_ROW_BLOCK = 256
MAX_SCORE_LOG2 = 60.0
N_SEG = 8
SEG_PAD = 4

OFF_GATE = 0
OFF_AX = 4096
OFF_AG = 5120
OFF_BQ = 6144
OFF_BK = 7168
OFF_BV = 8192
OFF_DV = 9216
OFF_DG = 10240
OFF_DQ = 11264
OFF_DK = 11776
OFF_C = 12288
C_WIDTH = 768
N_PROJ = OFF_C + C_WIDTH


def _cparams(sem):
    return pltpu.CompilerParams(dimension_semantics=sem, vmem_limit_bytes=VMEM_LIMIT)


def _sigmoid(x):
    return 1.0 / (1.0 + jnp.exp2(x * -LOG2E))


def _rms(x, g):
    return x * lax.rsqrt(jnp.mean(x * x, axis=-1, keepdims=True) + EPS) * g


def _dot(a, b):
    return jnp.dot(a, b, preferred_element_type=F32)


def _dot_nt(a, b):
    return lax.dot_general(a, b, (((1,), (1,)), ((), ())), preferred_element_type=F32)


def _dot_tn(a, b):
    return lax.dot_general(a, b, (((0,), (0,)), ((), ())), preferred_element_type=F32)


def _inproj_kernel(x_ref, g_ref, w_ref, o_ref):
    for r in range(x_ref.shape[0] // NORM_ROW_BLOCK):
        rows = slice(r * NORM_ROW_BLOCK, (r + 1) * NORM_ROW_BLOCK)
        xn = _rms(x_ref[rows, :], g_ref[...]).astype(BF16)
        o_ref[rows, :] = _dot(xn, w_ref[...]).astype(o_ref.dtype)


def _inproj(x2, g, w, tm, tn):
    m, k = x2.shape
    n = w.shape[1]
    return pl.pallas_call(
        _inproj_kernel,
        grid=(m // tm, n // tn),
        in_specs=[pl.BlockSpec((tm, k), lambda i, j: (i, 0)),
                  pl.BlockSpec((1, k), lambda i, j: (0, 0)),
                  pl.BlockSpec((k, tn), lambda i, j: (0, j))],
        out_specs=pl.BlockSpec((tm, tn), lambda i, j: (i, j)),
        out_shape=jax.ShapeDtypeStruct((m, n), BF16),
        compiler_params=_cparams(("parallel", "arbitrary")),
        name="inproj",
    )(x2, g, w)


def _gelu_tanh(x):
    c = 0.7978845608028654
    return (0.5 * x) * (1.0 + jnp.tanh(x * (c + (c * 0.044715) * (x * x))))


def _rglru_kernel(x_ref, ga_ref, cw_ref, cb_ref, wd_ref, bd_ref, lam_ref, o_ref,
                  xpad, a0, u0, a1, u1, *, seq):
    w = MXU_DIM
    nslab = w // LANES
    seg = seq // N_SEG
    pitch = seg + SEG_PAD
    left = CONV_WIDTH // 2

    xpad[0:8, :] = jnp.zeros((8, w), F32)
    xpad[seq + 8:seq + 16, :] = jnp.zeros((8, w), F32)
    for s in range(N_SEG):
        xpad[8 + s * seg:8 + (s + 1) * seg, :] = x_ref[s * seg:(s + 1) * seg, :].astype(F32)

    lam = lam_ref[...]
    nlam = -lam
    nls = (-LRU_C * LOG2E) * (jnp.maximum(nlam, 0.0) + jnp.log(1.0 + jnp.exp(-jnp.abs(nlam))))

    a_scr = (a0, a1)
    u_scr = (u0, u1)
    for s in range(N_SEG):
        base = 8 + s * seg - left
        xc = cb_ref[...]
        for t in range(CONV_WIDTH):
            xc = xc + xpad[base + t:base + t + seg, :] * cw_ref[t:t + 1, :]
        z = _dot(xc.astype(BF16), wd_ref[0]) + bd_ref[0]
        for d in range(2):
            r = _sigmoid(z[:, (2 * d) * w:(2 * d + 1) * w])
            gi = _sigmoid(z[:, (2 * d + 1) * w:(2 * d + 2) * w])
            a = jnp.exp2(nls[d:d + 1, :] * r)
            v = 1.0 - a * a
            u = (v * lax.rsqrt(jnp.maximum(v, 1e-30))) * (gi * xc)
            for sl in range(nslab):
                a_scr[d][sl, s * pitch:s * pitch + seg, :] = a[:, sl * LANES:(sl + 1) * LANES]
                u_scr[d][sl, s * pitch:s * pitch + seg, :] = u[:, sl * LANES:(sl + 1) * LANES]

    def rows(t):
        return pl.ds(t, N_SEG, stride=pitch)

    def scan_body(t, carry):
        new = []
        tb = seg - 1 - t
        for sl in range(nslab):
            hf, pf, hb, pb = carry[sl]
            af = a0[sl, rows(t), :]
            hf = af * hf + u0[sl, rows(t), :]
            pf = pf * af
            u0[sl, rows(t), :] = hf
            a0[sl, rows(t), :] = pf
            ab = a1[sl, rows(tb), :]
            hb = ab * hb + u1[sl, rows(tb), :]
            pb = pb * ab
            u1[sl, rows(tb), :] = hb
            a1[sl, rows(tb), :] = pb
            new.append((hf, pf, hb, pb))
        return tuple(new)

    zero = jnp.zeros((N_SEG, LANES), F32)
    one = jnp.ones((N_SEG, LANES), F32)
    fin = lax.fori_loop(0, seg, scan_body, tuple((zero, one, zero, one) for _ in range(nslab)),
                        unroll=4)

    cf, cb = [], []
    for sl in range(nslab):
        hf, pf, hb, pb = fin[sl]
        c = jnp.zeros((1, LANES), F32)
        cfr = [c]
        for s in range(1, N_SEG):
            c = pf[s - 1:s, :] * c + hf[s - 1:s, :]
            cfr.append(c)
        cf.append(jnp.concatenate(cfr, axis=0))
        c = jnp.zeros((1, LANES), F32)
        cbr = [c]
        for s in range(N_SEG - 2, -1, -1):
            c = pb[s + 1:s + 2, :] * c + hb[s + 1:s + 2, :]
            cbr.append(c)
        cb.append(jnp.concatenate(cbr[::-1], axis=0))

    def fix_body(t, carry):
        for sl in range(nslab):
            hf = u0[sl, rows(t), :] + a0[sl, rows(t), :] * cf[sl]
            hb = u1[sl, rows(t), :] + a1[sl, rows(t), :] * cb[sl]
            u0[sl, rows(t), :] = hf + hb
        return carry

    lax.fori_loop(0, seg, fix_body, 0, unroll=4)

    for s in range(N_SEG):
        g = _gelu_tanh(ga_ref[s * seg:(s + 1) * seg, :].astype(F32))
        for sl in range(nslab):
            h = u0[sl, s * pitch:s * pitch + seg, :]
            o_ref[s * seg:(s + 1) * seg, sl * LANES:(sl + 1) * LANES] = (
                h * g[:, sl * LANES:(sl + 1) * LANES]).astype(o_ref.dtype)


def _rglru(proj, cw, cb, wd, bd, lam, batch, seq):
    w = MXU_DIM
    nt = D_MODEL // w
    nslab = w // LANES
    rows = N_SEG * (seq // N_SEG + SEG_PAD)
    scan_scratch = pltpu.VMEM((nslab, rows, LANES), F32)
    return pl.pallas_call(
        functools.partial(_rglru_kernel, seq=seq),
        grid=(batch, nt),
        in_specs=[pl.BlockSpec((seq, w), lambda b, c: (b, OFF_AX // w + c)),
                  pl.BlockSpec((seq, w), lambda b, c: (b, OFF_AG // w + c)),
                  pl.BlockSpec((CONV_WIDTH, w), lambda b, c: (0, c)),
                  pl.BlockSpec((1, w), lambda b, c: (0, c)),
                  pl.BlockSpec((1, w, 4 * w), lambda b, c: (c, 0, 0)),
                  pl.BlockSpec((1, 1, 4 * w), lambda b, c: (c, 0, 0)),
                  pl.BlockSpec((2, w), lambda b, c: (0, c))],
        out_specs=pl.BlockSpec((seq, w), lambda b, c: (b, c)),
        out_shape=jax.ShapeDtypeStruct((batch * seq, D_MODEL), BF16),
        scratch_shapes=[pltpu.VMEM((seq + 16, w), F32),
                        scan_scratch, scan_scratch, scan_scratch, scan_scratch],
        compiler_params=_cparams(("parallel", "parallel")),
        name="rglru",
    )(proj, proj, cw, cb, wd, bd, lam)


def _rope_mxu(x, perm, c, s):
    return x * c + _dot(x.astype(BF16), perm) * s


def _rope_perm(width, rot, period):
    half = rot // 2
    lane = jnp.arange(width)
    j = lane % period
    src = jnp.where(j < half, lane + half, lane - half)
    sign = jnp.where(j < half, -1.0, 1.0) * (j < rot)
    return ((lane[:, None] == src[None, :]) * sign[None, :]).astype(BF16)


def _rope_cs_tables(seq, rot, period, theta):
    pos = jnp.arange(seq, dtype=F32)
    inv_freq = theta ** (-jnp.arange(0, rot, 2, dtype=F32) / rot)
    ang = pos[:, None] * inv_freq[None, :]
    cos, sin = jnp.cos(ang), jnp.sin(ang)
    pad = period - rot
    c = jnp.concatenate([cos, cos, jnp.ones((seq, pad), F32)], axis=1)
    s = jnp.concatenate([sin, sin, jnp.zeros((seq, pad), F32)], axis=1)
    rep = LANES // period
    return jnp.tile(c, (1, rep)), jnp.tile(s, (1, rep))


def _diff_prep_kernel(q_ref, k_ref, gq_ref, gk_ref, c_ref, s_ref, gm_ref, rp_ref,
                      qo_ref, ko_ref, *, qscale):
    c = jnp.concatenate([c_ref[...]] * (MXU_DIM // LANES), axis=1)
    sn = jnp.concatenate([s_ref[...]] * (MXU_DIM // LANES), axis=1)
    gm = gm_ref[...]
    rp = rp_ref[...]
    for src, g_ref, is_key in ((q_ref, gq_ref, False), (k_ref, gk_ref, True)):
        for blk in range(D_MODEL // MXU_DIM):
            x = src[:, blk * MXU_DIM:(blk + 1) * MXU_DIM].astype(F32)
            ms = _dot((x * x).astype(BF16), gm)
            xr = _rope_mxu(x * lax.rsqrt(ms + EPS) * g_ref[...], rp, c, sn)
            for h in range(MXU_DIM // LANES):
                xb = xr[:, h * LANES:(h + 1) * LANES]
                lo = blk * MXU_DIM + h * LANES
                if is_key:
                    ko_ref[lo:lo + LANES, :] = xb.T.astype(ko_ref.dtype)
                else:
                    qo_ref[:, lo:lo + LANES] = (xb * qscale).astype(qo_ref.dtype)


def _diff_prep(proj, gq, gk, tabs, gm, rp, seq, tm, qscale):
    m = proj.shape[0]
    nrow = seq // tm
    tab_spec = pl.BlockSpec((tm, LANES), lambda i: (i % nrow, 0))
    return pl.pallas_call(
        functools.partial(_diff_prep_kernel, qscale=qscale),
        grid=(m // tm,),
        in_specs=[pl.BlockSpec((tm, D_MODEL), lambda i: (i, OFF_BQ // D_MODEL)),
                  pl.BlockSpec((tm, D_MODEL), lambda i: (i, OFF_BK // D_MODEL)),
                  pl.BlockSpec((1, MXU_DIM), lambda i: (0, 0)),
                  pl.BlockSpec((1, MXU_DIM), lambda i: (0, 0)),
                  tab_spec, tab_spec,
                  pl.BlockSpec((MXU_DIM, MXU_DIM), lambda i: (0, 0)),
                  pl.BlockSpec((MXU_DIM, MXU_DIM), lambda i: (0, 0))],
        out_specs=[pl.BlockSpec((tm, D_MODEL), lambda i: (i, 0)),
                   pl.BlockSpec((None, D_MODEL, tm), lambda i: (i // nrow, 0, i % nrow))],
        out_shape=[jax.ShapeDtypeStruct((m, D_MODEL), BF16),
                   jax.ShapeDtypeStruct((m // seq, D_MODEL, seq), BF16)],
        compiler_params=_cparams(("parallel",)),
        name="diff_prep",
    )(proj, proj, gq, gk, *tabs, gm, rp)


def _mla_prep_kernel(c_ref, qag_ref, wuq_ref, kvag_ref, wukv_ref, gq_ref, gk_ref,
                     c_tab, s_tab, rp_ref, qo_ref, ko_ref, vo_ref, *, qscale):
    dh = MLA_NOPE + MLA_ROPE
    ct, st, rp = c_tab[...], s_tab[...], rp_ref[...]

    def sumsq(x):
        return _dot((x * x).astype(BF16), jnp.ones((x.shape[1], LANES), BF16))

    def rms_wide(x, g):
        inv = lax.rsqrt(sumsq(x) * (1.0 / x.shape[1]) + EPS)
        return x * jnp.concatenate([inv] * (x.shape[1] // LANES), axis=1) * g

    cq = c_ref[:, 0:MLA_Q_LORA].astype(F32)
    ckv = c_ref[:, MLA_Q_LORA:MLA_Q_LORA + MLA_KV_LORA].astype(F32)
    kpe = c_ref[:, MLA_Q_LORA + MLA_KV_LORA:C_WIDTH].astype(F32)
    q = _dot(rms_wide(cq, qag_ref[...]).astype(BF16), wuq_ref[...])
    kv = _dot(rms_wide(ckv, kvag_ref[...]).astype(BF16), wukv_ref[...])
    gq_n, gq_r = gq_ref[:, 0:LANES], gq_ref[:, LANES:2 * LANES]
    gk_n, gk_r = gk_ref[:, 0:LANES], gk_ref[:, LANES:2 * LANES]
    kpe_ss = sumsq(kpe)
    for h in range(MLA_HEADS):
        lo = h * MLA_HEAD_PAD
        qn, qr = q[:, lo:lo + LANES], q[:, lo + LANES:lo + 2 * LANES]
        qh = q[:, lo:lo + 2 * LANES]
        inv = lax.rsqrt(jnp.sum(qh * qh, axis=-1, keepdims=True) * (1.0 / dh) + EPS) * qscale
        qo_ref[:, lo:lo + LANES] = (qn * inv * gq_n).astype(qo_ref.dtype)
        qo_ref[:, lo + LANES:lo + 2 * LANES] = _rope_mxu(
            qr * inv * gq_r, rp, ct, st).astype(qo_ref.dtype)
        kn = kv[:, h * LANES:(h + 1) * LANES]
        inv = lax.rsqrt((jnp.sum(kn * kn, axis=-1, keepdims=True) + kpe_ss) * (1.0 / dh) + EPS)
        ko_ref[lo:lo + LANES, :] = (kn * inv * gk_n).T.astype(ko_ref.dtype)
        ko_ref[lo + LANES:lo + 2 * LANES, :] = _rope_mxu(
            kpe * inv * gk_r, rp, ct, st).T.astype(ko_ref.dtype)
    vo_ref[...] = kv[:, MLA_HEADS * LANES:].astype(vo_ref.dtype)


def _mla_prep(proj, qag, wuq, kvag, wukv, gq, gk, tabs, rp, seq, tm, qscale):
    m = proj.shape[0]
    nrow = seq // tm
    hp = MLA_HEADS * MLA_HEAD_PAD
    tab_spec = pl.BlockSpec((tm, LANES), lambda i: (i % nrow, 0))
    const = lambda shape: pl.BlockSpec(shape, lambda i: (0, 0))
    return pl.pallas_call(
        functools.partial(_mla_prep_kernel, qscale=qscale),
        grid=(m // tm,),
        in_specs=[pl.BlockSpec((tm, C_WIDTH), lambda i: (i, OFF_C // C_WIDTH)),
                  const((1, MLA_Q_LORA)), const((MLA_Q_LORA, hp)),
                  const((1, MLA_KV_LORA)), const((MLA_KV_LORA, 2 * MLA_HEADS * LANES)),
                  const((1, MLA_HEAD_PAD)), const((1, MLA_HEAD_PAD)),
                  tab_spec, tab_spec, const((LANES, LANES))],
        out_specs=[pl.BlockSpec((tm, hp), lambda i: (i, 0)),
                   pl.BlockSpec((None, hp, tm), lambda i: (i // nrow, 0, i % nrow)),
                   pl.BlockSpec((tm, MLA_HEADS * MLA_V), lambda i: (i, 0))],
        out_shape=[jax.ShapeDtypeStruct((m, hp), BF16),
                   jax.ShapeDtypeStruct((m // seq, hp, seq), BF16),
                   jax.ShapeDtypeStruct((m, MLA_HEADS * MLA_V), BF16)],
        compiler_params=_cparams(("parallel",)),
        name="mla_prep",
    )(proj, qag, wuq, kvag, wukv, gq, gk, *tabs, rp)


def _attn_kernel(q_ref, k_ref, v_ref, lam_ref, sg_ref, o_ref, vext, m_scr, acc_scr,
                 *, seq, tk, diff, out_scale):
    tq = q_ref.shape[0]

    @pl.when(pl.program_id(2) == 0)
    def _():
        vext[:, 0:LANES] = v_ref[...]
        vext[:, LANES:2 * LANES] = jnp.ones((seq, LANES), BF16)

    q = q_ref[...]
    if diff:
        lane = lax.broadcasted_iota(jnp.int32, q.shape, 1)
        zero = jnp.zeros_like(q)
        q = jnp.concatenate([jnp.where(lane < DIFF_DH, q, zero),
                             jnp.where(lane >= DIFF_DH, q, zero)], axis=0)
    rows = q.shape[0]
    m_scr[...] = jnp.full((rows, LANES), -1e30, F32)
    acc_scr[...] = jnp.zeros((rows, 2 * LANES), F32)

    def body(j, carry):
        off = pl.multiple_of(j * tk, tk)
        s = _dot(q, k_ref[:, pl.ds(off, tk)])
        m_prev = m_scr[...]
        m_new = jnp.maximum(m_prev, jnp.max(s, axis=1, keepdims=True))
        alpha = jnp.exp2(m_prev - m_new)
        p = jnp.exp2(s - m_new[:, 0:1]).astype(BF16)
        acc_scr[...] = (acc_scr[...] * jnp.concatenate([alpha, alpha], axis=1)
                        + _dot(p, vext[pl.ds(off, tk), :]))
        m_scr[...] = m_new
        return carry

    lax.fori_loop(0, seq // tk, body, 0)

    acc = acc_scr[...]
    o = acc[:, 0:LANES] / acc[:, LANES:2 * LANES]
    if diff:
        o = o[0:tq, :] - lam_ref[...] * o[tq:2 * tq, :]
        o = _rms(o, sg_ref[...]) * out_scale
    o_ref[...] = o.astype(o_ref.dtype)


def _attn_bounded_kernel(q_ref, k_ref, v_ref, lam_ref, sg_ref, o_ref, vext,
                         *, seq, tk, diff, out_scale):
    tq = q_ref.shape[0]

    @pl.when(pl.program_id(2) == 0)
    def _():
        vext[:, 0:LANES] = v_ref[...]
        vext[:, LANES:2 * LANES] = jnp.ones((seq, LANES), BF16)

    q = q_ref[...]
    if diff:
        lane = lax.broadcasted_iota(jnp.int32, q.shape, 1)
        zero = jnp.zeros_like(q)
        q = jnp.concatenate([jnp.where(lane < DIFF_DH, q, zero),
                             jnp.where(lane >= DIFF_DH, q, zero)], axis=0)
    rb = ATTN_ROW_BLOCK

    def softmax_rows(lo):
        acc = None
        for j in range(seq // tk):
            p = jnp.exp2(_dot(q[lo:lo + rb, :], k_ref[:, j * tk:(j + 1) * tk])).astype(BF16)
            pv = _dot(p, vext[j * tk:(j + 1) * tk, :])
            acc = pv if acc is None else acc + pv
        return acc[:, 0:LANES] / acc[:, LANES:2 * LANES]

    for r in range(tq // rb):
        o = softmax_rows(r * rb)
        if diff:
            o = o - lam_ref[...] * softmax_rows(tq + r * rb)
            o = _rms(o, sg_ref[...]) * out_scale
        o_ref[r * rb:(r + 1) * rb, :] = o.astype(o_ref.dtype)


def _score_bound(gq, gk, dh):
    return 1.02 * LOG2E * math.sqrt(dh) * jnp.max(jnp.abs(gq)) * jnp.max(jnp.abs(gk))


def _attention(q, k, v, lam, sg, *, v_off, batch, seq, heads, dq, tq, tk, diff, out_scale, bounded):
    nq = seq // tq
    rows = 2 * tq if diff else tq
    vblk = v_off // LANES
    body = _attn_bounded_kernel if bounded else _attn_kernel
    return pl.pallas_call(
        functools.partial(body, seq=seq, tk=tk, diff=diff, out_scale=out_scale),
        grid=(batch, heads, nq),
        in_specs=[pl.BlockSpec((tq, dq), lambda b, h, i: (b * nq + i, h)),
                  pl.BlockSpec((None, dq, seq), lambda b, h, i: (b, h, 0)),
                  pl.BlockSpec((seq, LANES), lambda b, h, i: (b, vblk + h)),
                  pl.BlockSpec((1, LANES), lambda b, h, i: (0, 0)),
                  pl.BlockSpec((1, LANES), lambda b, h, i: (0, 0))],
        out_specs=pl.BlockSpec((tq, LANES), lambda b, h, i: (b * nq + i, h)),
        out_shape=jax.ShapeDtypeStruct((batch * seq, heads * LANES), BF16),
        scratch_shapes=[pltpu.VMEM((seq, 2 * LANES), BF16)] + ([] if bounded else [
            pltpu.VMEM((rows, LANES), F32), pltpu.VMEM((rows, 2 * LANES), F32)]),
        compiler_params=_cparams(("parallel", "parallel", "arbitrary")),
        name=("diff_attn" if diff else "mla_attn") + ("_bounded" if bounded else "_online"),
    )(q, k, v, lam, sg)


def _retention_kernel(q_ref, k_ref, v_ref, g_ref, lg_ref, gn_ref, c_tab, s_tab, rp_ref,
                      o_ref, qs, kts, o_fwd, o_bwd, state, *, seq):
    ch = RET_CHUNK if seq % RET_CHUNK == 0 else seq
    nc = seq // ch
    ct, st, rp = c_tab[...], s_tab[...], rp_ref[...]
    qs[...] = _rope_mxu(q_ref[...].astype(F32), rp, ct, st)
    kt = (_rope_mxu(k_ref[...].astype(F32), rp, ct, st) * (RET_DK ** -0.5)).T
    row = lax.broadcasted_iota(jnp.int32, kt.shape, 0)
    kts[0] = jnp.where(row < RET_DK, kt, 0.0)
    kts[1] = jnp.where(row >= RET_DK, kt, 0.0)
    state[...] = jnp.zeros(state.shape, F32)

    ri = lax.broadcasted_iota(jnp.int32, (ch, ch), 0).astype(F32)
    ci = lax.broadcasted_iota(jnp.int32, (ch, ch), 1).astype(F32)
    qidx = lax.broadcasted_iota(jnp.int32, (ch, 1), 0).astype(F32)
    kidx = lax.broadcasted_iota(jnp.int32, (1, ch), 1).astype(F32)
    dec = []
    for hh in range(2):
        lg = lg_ref[hh:hh + 1, 0:1]
        dec.append(dict(
            intra=jnp.exp(lg * jnp.abs(ri - ci)),
            tail=jnp.exp(lg * (ch - 1.0 - kidx)),
            head=jnp.exp(lg * kidx),
            qf=jnp.exp(lg * (qidx + 1.0)),
            qb=jnp.exp(lg * (ch - qidx)),
            chunk=jnp.exp(lg * float(ch))))

    def sweep(t, carry):
        of = pl.multiple_of(t * ch, ch)
        ob = pl.multiple_of((nc - 1 - t) * ch, ch)
        qf = qs[pl.ds(of, ch), :]
        qb = qs[pl.ds(ob, ch), :]
        for hh in range(2):
            d = dec[hh]
            vlo = hh * RET_DV
            ktf = kts[hh, :, pl.ds(of, ch)]
            vf = v_ref[pl.ds(of, ch), vlo:vlo + RET_DV]
            sc = _dot(qf.astype(BF16), ktf.astype(BF16)) * d["intra"]
            past = state[2 * hh]
            o_fwd[hh, pl.ds(of, ch), :] = (_dot(sc.astype(BF16), vf)
                                           + _dot((qf * d["qf"]).astype(BF16), past.astype(BF16)))
            state[2 * hh] = past * d["chunk"] + _dot((ktf * d["tail"]).astype(BF16), vf)
            ktb = kts[hh, :, pl.ds(ob, ch)]
            vb = v_ref[pl.ds(ob, ch), vlo:vlo + RET_DV]
            fut = state[2 * hh + 1]
            o_bwd[hh, pl.ds(ob, ch), :] = _dot((qb * d["qb"]).astype(BF16), fut.astype(BF16))
            state[2 * hh + 1] = fut * d["chunk"] + _dot((ktb * d["head"]).astype(BF16), vb)
        return carry

    lax.fori_loop(0, nc, sweep, 0, unroll=2 if nc % 2 == 0 else 1)

    lane_mean = jnp.full((RET_DV, RET_DV), 1.0 / RET_DV, BF16)

    def fin(n, carry):
        off = pl.multiple_of(n * ch, ch)
        for hh in range(2):
            vlo = hh * RET_DV
            o = o_fwd[hh, pl.ds(off, ch), :] + o_bwd[hh, pl.ds(off, ch), :]
            oc = o - _dot(o.astype(BF16), lane_mean)
            var = _dot((oc * oc).astype(BF16), lane_mean)
            y = oc * lax.rsqrt(var + EPS) * gn_ref[...]
            g = g_ref[pl.ds(off, ch), vlo:vlo + RET_DV].astype(F32)
            o_ref[pl.ds(off, ch), vlo:vlo + RET_DV] = (g * _sigmoid(g) * y).astype(o_ref.dtype)
        return carry

    lax.fori_loop(0, nc, fin, 0, unroll=2 if nc % 2 == 0 else 1)


def _retention(proj, lg, gn, tabs, rp, batch, seq):
    npair = RET_HEADS // 2
    pw = 2 * RET_DV
    tab_spec = pl.BlockSpec((seq, LANES), lambda b, j: (0, 0))
    return pl.pallas_call(
        functools.partial(_retention_kernel, seq=seq),
        grid=(batch, npair),
        in_specs=[pl.BlockSpec((seq, LANES), lambda b, j: (b, OFF_DQ // LANES + j)),
                  pl.BlockSpec((seq, LANES), lambda b, j: (b, OFF_DK // LANES + j)),
                  pl.BlockSpec((seq, pw), lambda b, j: (b, OFF_DV // pw + j)),
                  pl.BlockSpec((seq, pw), lambda b, j: (b, OFF_DG // pw + j)),
                  pl.BlockSpec((None, 2, LANES), lambda b, j: (j, 0, 0)),
                  pl.BlockSpec((1, RET_DV), lambda b, j: (0, 0)),
                  tab_spec, tab_spec, pl.BlockSpec((LANES, LANES), lambda b, j: (0, 0))],
        out_specs=pl.BlockSpec((seq, pw), lambda b, j: (b, j)),
        out_shape=jax.ShapeDtypeStruct((batch * seq, RET_HEADS * RET_DV), BF16),
        scratch_shapes=[pltpu.VMEM((seq, LANES), F32),
                        pltpu.VMEM((2, LANES, seq), F32),
                        pltpu.VMEM((2, seq, RET_DV), F32),
                        pltpu.VMEM((2, seq, RET_DV), F32),
                        pltpu.VMEM((4, LANES, RET_DV), F32)],
        compiler_params=_cparams(("parallel", "parallel")),
        name="retention",
    )(proj, proj, proj, proj, lg, gn, *tabs, rp)


def _merge_kernel(x_ref, gl_ref, gb_ref, ya_ref, yb_ref, yc_ref, yd_ref,
                  wa_ref, wb_ref, wc_ref, wd_ref, wo_ref, o_ref):
    for r in range(x_ref.shape[0] // NORM_ROW_BLOCK):
        rows = slice(r * NORM_ROW_BLOCK, (r + 1) * NORM_ROW_BLOCK)
        merged = None
        for n, (y_ref, w_ref) in enumerate(((ya_ref, wa_ref), (yb_ref, wb_ref),
                                            (yc_ref, wc_ref), (yd_ref, wd_ref))):
            lo = n * D_MODEL
            gate = _sigmoid(gl_ref[rows, lo:lo + D_MODEL].astype(F32)
                            + gb_ref[:, lo:lo + D_MODEL])
            term = gate * _dot(y_ref[rows, :], w_ref[...])
            merged = term if merged is None else merged + term
        o_ref[rows, :] = x_ref[rows, :] + _dot(merged.astype(BF16), wo_ref[...])


def _merge(x2, proj, gb, ys, ws, wo, tm):
    m = x2.shape[0]
    row = lambda width: pl.BlockSpec((tm, width), lambda i: (i, 0))
    wspec = pl.BlockSpec((D_MODEL, D_MODEL), lambda i: (0, 0), pipeline_mode=pl.Buffered(1))
    return pl.pallas_call(
        _merge_kernel,
        grid=(m // tm,),
        in_specs=[row(D_MODEL), row(N_BRANCH * D_MODEL),
                  pl.BlockSpec((1, N_BRANCH * D_MODEL), lambda i: (0, 0)),
                  row(D_MODEL), row(D_MODEL), row(D_MODEL), row(D_MODEL),
                  wspec, wspec, wspec, wspec, wspec],
        out_specs=row(D_MODEL),
        out_shape=jax.ShapeDtypeStruct((m, D_MODEL), F32),
        compiler_params=_cparams(("parallel",)),
        name="merge",
    )(x2, proj, gb, *ys, *ws, wo)


def _ffn_kernel(x_ref, g2_ref, w1_ref, w2_ref, g3_ref, wpg_ref, p_ref, wpe_ref, o_ref,
                acc_ref):
    j = pl.program_id(1)
    blocks = [slice(r * NORM_ROW_BLOCK, (r + 1) * NORM_ROW_BLOCK)
              for r in range(x_ref.shape[0] // NORM_ROW_BLOCK)]

    @pl.when(j == 0)
    def _():
        acc_ref[...] = jnp.zeros_like(acc_ref)

    for rows in blocks:
        xn = _rms(x_ref[rows, :], g2_ref[...]).astype(BF16)
        h = jnp.maximum(_dot(xn, w1_ref[...]), 0.0)
        acc_ref[rows, :] += _dot((h * h).astype(BF16), w2_ref[...])

    @pl.when(j == pl.num_programs(1) - 1)
    def _():
        for rows in blocks:
            x = x_ref[rows, :] + acc_ref[rows, :]
            gate = _sigmoid(_dot(_rms(x, g3_ref[...]).astype(BF16), wpg_ref[...]))
            o_ref[rows, :] = x + gate * _dot(p_ref[rows, :].astype(BF16), wpe_ref[...])


def _ffn(x2, g2, w1, w2, g3, wpg, p2, layer, wpe, tm, tf):
    m = x2.shape[0]
    dff = w1.shape[1]
    pdim = p2.shape[1]
    p_row0 = layer * (m // tm)
    return pl.pallas_call(
        _ffn_kernel,
        grid=(m // tm, dff // tf),
        in_specs=[pl.BlockSpec((tm, D_MODEL), lambda i, j: (i, 0)),
                  pl.BlockSpec((1, D_MODEL), lambda i, j: (0, 0)),
                  pl.BlockSpec((D_MODEL, tf), lambda i, j: (0, j)),
                  pl.BlockSpec((tf, D_MODEL), lambda i, j: (j, 0)),
                  pl.BlockSpec((1, D_MODEL), lambda i, j: (0, 0)),
                  pl.BlockSpec((D_MODEL, D_MODEL), lambda i, j: (0, 0)),
                  pl.BlockSpec((tm, pdim), lambda i, j: (p_row0 + i, 0)),
                  pl.BlockSpec((pdim, D_MODEL), lambda i, j: (0, 0))],
        out_specs=pl.BlockSpec((tm, D_MODEL), lambda i, j: (i, 0)),
        out_shape=jax.ShapeDtypeStruct((m, D_MODEL), F32),
        scratch_shapes=[pltpu.VMEM((tm, D_MODEL), F32)],
        compiler_params=_cparams(("parallel", "arbitrary")),
        name="ffn_ple",
    )(x2, g2, w1, w2, g3, wpg, p2, wpe)


def _block_diag(blocks):
    n, c, d = blocks.shape
    eye = jnp.eye(n, dtype=blocks.dtype)
    return jnp.einsum('ncd,nm->ncmd', blocks, eye).reshape(n * c, n * d)


def _permute_w_in(w_in):
    sizes = (1024, 1024, 1024, 1024, 1024, 384, 256, 64, 512, 512, 1024, 1024, 4096)
    offs = [0]
    for s in sizes:
        offs.append(offs[-1] + s)
    (a_x, a_g, b_q, b_k, b_v, c_q, c_kv, c_kpe, d_q, d_k, d_v, d_g, gates) = [
        w_in[:, offs[n]:offs[n + 1]] for n in range(len(sizes))]
    pad = jnp.zeros((w_in.shape[0], C_WIDTH - 704), w_in.dtype)
    return jnp.concatenate([gates, a_x, a_g, b_q, b_k, b_v, d_v, d_g, d_q, d_k,
                            c_q, c_kv, c_kpe, pad], axis=1).astype(BF16)


def _lru_weights(wa, ba, wx, bx):
    w = MXU_DIM
    nt = D_MODEL // w
    dense = [_block_diag(m) for m in (wa[0], wx[0], wa[1], wx[1])]
    wd = jnp.stack([jnp.concatenate([d[c * w:(c + 1) * w, c * w:(c + 1) * w] for d in dense], axis=1)
                    for c in range(nt)]).astype(BF16)
    bias = (ba[0], bx[0], ba[1], bx[1])
    bd = jnp.stack([jnp.concatenate([b[c * w:(c + 1) * w] for b in bias])[None, :]
                    for c in range(nt)])
    return wd, bd


def _mla_weights(wuq, wukv, gq, gk):
    dh = MLA_NOPE + MLA_ROPE
    zpad = MLA_HEAD_PAD - dh
    wq = wuq.reshape(MLA_Q_LORA, MLA_HEADS, dh)
    wq = jnp.pad(wq, ((0, 0), (0, 0), (0, zpad))).reshape(MLA_Q_LORA, MLA_HEADS * MLA_HEAD_PAD)
    wkv = wukv.reshape(MLA_KV_LORA, MLA_HEADS, MLA_NOPE + MLA_V)
    wk = wkv[:, :, :MLA_NOPE].reshape(MLA_KV_LORA, MLA_HEADS * MLA_NOPE)
    wv = wkv[:, :, MLA_NOPE:].reshape(MLA_KV_LORA, MLA_HEADS * MLA_V)
    gqp = jnp.pad(gq, (0, zpad))[None, :]
    gkp = jnp.pad(gk, (0, zpad))[None, :]
    return wq.astype(BF16), jnp.concatenate([wk, wv], axis=1).astype(BF16), gqp, gkp


def kernel(x, p, norm1_g, w_in, gate_b, conv_w, conv_b, lru_wa, lru_ba, lru_wx, lru_bx, lru_lambda,
           diff_q_g, diff_k_g, diff_lam, diff_sub_g, mla_qa_g, mla_wuq, mla_kva_g, mla_wukv,
           mla_q_g, mla_k_g, ret_gn_g, w_br_a, w_br_b, w_br_c, w_br_d, w_out, norm2_g,
           w_ff1, w_ff2, norm3_g, w_ple_gate, w_ple_proj):
    batch, seq, d_model = x.shape
    depth = w_in.shape[0]
    assert d_model == D_MODEL and seq % (8 * N_SEG) == 0
    m = batch * seq
    tm_big = min(1024, seq)
    tm_small = min(512, seq)
    tq = min(1024, seq)
    tk = min(512, seq)

    tabs_b = _rope_cs_tables(seq, DIFF_ROT, DIFF_DH, ROPE_THETA)
    tabs_c = _rope_cs_tables(seq, MLA_ROPE, LANES, ROPE_THETA)
    rp_b = _rope_perm(MXU_DIM, DIFF_ROT, DIFF_DH)
    rp_c = _rope_perm(LANES, MLA_ROPE, LANES)
    tabs_d = _rope_cs_tables(seq, RET_DK, RET_DK, RET_THETA)
    rp_d = _rope_perm(LANES, RET_DK, RET_DK)
    log_g = jnp.log1p(-jnp.exp2(-5.0 - jnp.arange(RET_HEADS, dtype=F32)))
    lg = jnp.broadcast_to(log_g.reshape(RET_HEADS // 2, 2, 1), (RET_HEADS // 2, 2, LANES))
    grp = jnp.arange(MXU_DIM) // DIFF_DH
    gm = ((grp[:, None] == grp[None, :]).astype(F32) / DIFF_DH).astype(BF16)
    ones_row = jnp.ones((1, LANES), F32)

    x2 = x.reshape(m, d_model)
    for i in range(depth):
        proj = _inproj(x2, norm1_g[i][None, :], _permute_w_in(w_in[i]), tm_big, N_PROJ // 3)

        wd, bd = _lru_weights(lru_wa[i], lru_ba[i], lru_wx[i], lru_bx[i])
        y_a = _rglru(proj, conv_w[i], conv_b[i][None, :], wd, bd, lru_lambda[i], batch, seq)

        lam_init = 0.8 - 0.6 * math.exp(-0.3 * i)
        lp = diff_lam[i].astype(F32)
        lam = jnp.exp(jnp.sum(lp[0] * lp[1])) - jnp.exp(jnp.sum(lp[2] * lp[3])) + lam_init
        gq = jnp.tile(diff_q_g[i], MXU_DIM // DIFF_DH)[None, :]
        gk = jnp.tile(diff_k_g[i], MXU_DIM // DIFF_DH)[None, :]
        qb, kb = _diff_prep(proj, gq, gk, tabs_b, gm, rp_b, seq, tm_small, DIFF_DH ** -0.5 * LOG2E)
        attn_b = functools.partial(_attention, v_off=OFF_BV, batch=batch, seq=seq, heads=DIFF_HEADS,
                                   dq=LANES, tq=tq, tk=tk, diff=True, out_scale=1.0 - lam_init)
        y_b = lax.cond(
            _score_bound(diff_q_g[i], diff_k_g[i], DIFF_DH) <= MAX_SCORE_LOG2,
            functools.partial(attn_b, bounded=True), functools.partial(attn_b, bounded=False),
            qb, kb, proj, jnp.full((1, LANES), lam, F32), diff_sub_g[i][None, :])

        wq, wkv, gqp, gkp = _mla_weights(mla_wuq[i], mla_wukv[i], mla_q_g[i], mla_k_g[i])
        qc, kc, vc = _mla_prep(proj, mla_qa_g[i][None, :], wq, mla_kva_g[i][None, :], wkv,
                               gqp, gkp, tabs_c, rp_c, seq, tm_small,
                               (MLA_NOPE + MLA_ROPE) ** -0.5 * LOG2E)
        attn_c = functools.partial(_attention, v_off=0, batch=batch, seq=seq, heads=MLA_HEADS,
                                   dq=MLA_HEAD_PAD, tq=min(2 * tq, seq), tk=tk, diff=False,
                                   out_scale=1.0)
        y_c = lax.cond(
            _score_bound(mla_q_g[i], mla_k_g[i], MLA_NOPE + MLA_ROPE) <= MAX_SCORE_LOG2,
            functools.partial(attn_c, bounded=True), functools.partial(attn_c, bounded=False),
            qc, kc, vc, ones_row, ones_row)

        y_d = _retention(proj, lg, ret_gn_g[i][None, :], tabs_d, rp_d, batch, seq)

        x2 = _merge(x2, proj, gate_b[i].reshape(1, N_BRANCH * D_MODEL), (y_a, y_b, y_c, y_d),
                    tuple(w[i].astype(BF16) for w in (w_br_a, w_br_b, w_br_c, w_br_d)),
                    w_out[i].astype(BF16), tm_small)

        x2 = _ffn(x2, norm2_g[i][None, :], w_ff1[i].astype(BF16), w_ff2[i].astype(BF16),
                  norm3_g[i][None, :], w_ple_gate[i].astype(BF16),
                  p.reshape(depth * m, p.shape[-1]), i, w_ple_proj[i].astype(BF16), tm_big, 2048)
    return x2.reshape(batch, seq, d_model)
```

```python
import functools
import math

import jax
import jax.numpy as jnp
from jax import lax
from jax.experimental import pallas as pl
from jax.experimental.pallas import tpu as pltpu

F32 = jnp.float32
BF16 = jnp.bfloat16

EPS = 1e-6
ROPE_THETA = 500000.0
RET_THETA = 10000.0
LRU_C = 8.0
LOG2E = 1.4426950408889634

LANES = 128
MXU_DIM = 256
VMEM_LIMIT = 56 * 1024 * 1024

D_MODEL = 1024
N_BRANCH = 4
LRU_BLOCKS = 16
CONV_WIDTH = 4
DIFF_HEADS = 8
DIFF_DH = 64
DIFF_ROT = 16
MLA_HEADS = 8
MLA_NOPE = 128
MLA_ROPE = 64
MLA_V = 128
MLA_Q_LORA = 384
MLA_KV_LORA = 256
MLA_HEAD_PAD = 256
RET_HEADS = 8
RET_DK = 64
RET_DV = 128
RET_CHUNK = 256
ATTN_ROW_BLOCK = 128
NORM_ROW_BLOCK = 256
MAX_SCORE_LOG2 = 60.0
N_SEG = 8
SEG_PAD = 4

OFF_GATE = 0
OFF_AX = 4096
OFF_AG = 5120
OFF_BQ = 6144
OFF_BK = 7168
OFF_BV = 8192
OFF_DV = 9216
OFF_DG = 10240
OFF_DQ = 11264
OFF_DK = 11776
OFF_C = 12288
C_WIDTH = 768
N_PROJ = OFF_C + C_WIDTH


def _cparams(sem):
    return pltpu.CompilerParams(dimension_semantics=sem, vmem_limit_bytes=VMEM_LIMIT)


def _sigmoid(x):
    return 1.0 / (1.0 + jnp.exp2(x * -LOG2E))


def _rms(x, g):
    return x * lax.rsqrt(jnp.mean(x * x, axis=-1, keepdims=True) + EPS) * g


def _dot(a, b):
    return jnp.dot(a, b, preferred_element_type=F32)


def _dot_nt(a, b):
    return lax.dot_general(a, b, (((1,), (1,)), ((), ())), preferred_element_type=F32)


def _dot_tn(a, b):
    return lax.dot_general(a, b, (((0,), (0,)), ((), ())), preferred_element_type=F32)


def _inproj_kernel(x_ref, g_ref, w_ref, o_ref):
    for r in range(x_ref.shape[0] // NORM_ROW_BLOCK):
        rows = slice(r * NORM_ROW_BLOCK, (r + 1) * NORM_ROW_BLOCK)
        xn = _rms(x_ref[rows, :], g_ref[...]).astype(BF16)
        o_ref[rows, :] = _dot(xn, w_ref[...]).astype(o_ref.dtype)


def _inproj(x2, g, w, tm, tn):
    m, k = x2.shape
    n = w.shape[1]
    return pl.pallas_call(
        _inproj_kernel,
        grid=(m // tm, n // tn),
        in_specs=[pl.BlockSpec((tm, k), lambda i, j: (i, 0)),
                  pl.BlockSpec((1, k), lambda i, j: (0, 0)),
                  pl.BlockSpec((k, tn), lambda i, j: (0, j))],
        out_specs=pl.BlockSpec((tm, tn), lambda i, j: (i, j)),
        out_shape=jax.ShapeDtypeStruct((m, n), BF16),
        compiler_params=_cparams(("parallel", "arbitrary")),
        name="inproj",
    )(x2, g, w)


def _gelu_tanh(x):
    c = 0.7978845608028654
    return (0.5 * x) * (1.0 + jnp.tanh(x * (c + (c * 0.044715) * (x * x))))


def _rglru_kernel(x_ref, ga_ref, cw_ref, cb_ref, wd_ref, bd_ref, lam_ref, o_ref,
                  xpad, a0, u0, a1, u1, *, seq):
    w = MXU_DIM
    nslab = w // LANES
    seg = seq // N_SEG
    pitch = seg + SEG_PAD
    left = CONV_WIDTH // 2

    xpad[0:8, :] = jnp.zeros((8, w), F32)
    xpad[seq + 8:seq + 16, :] = jnp.zeros((8, w), F32)
    for s in range(N_SEG):
        xpad[8 + s * seg:8 + (s + 1) * seg, :] = x_ref[s * seg:(s + 1) * seg, :].astype(F32)

    lam = lam_ref[...]
    nlam = -lam
    nls = (-LRU_C * LOG2E) * (jnp.maximum(nlam, 0.0) + jnp.log(1.0 + jnp.exp(-jnp.abs(nlam))))

    a_scr = (a0, a1)
    u_scr = (u0, u1)
    for s in range(N_SEG):
        base = 8 + s * seg - left
        xc = cb_ref[...]
        for t in range(CONV_WIDTH):
            xc = xc + xpad[base + t:base + t + seg, :] * cw_ref[t:t + 1, :]
        z = _dot(xc.astype(BF16), wd_ref[0]) + bd_ref[0]
        for d in range(2):
            r = _sigmoid(z[:, (2 * d) * w:(2 * d + 1) * w])
            gi = _sigmoid(z[:, (2 * d + 1) * w:(2 * d + 2) * w])
            a = jnp.exp2(nls[d:d + 1, :] * r)
            v = 1.0 - a * a
            u = (v * lax.rsqrt(jnp.maximum(v, 1e-30))) * (gi * xc)
            for sl in range(nslab):
                a_scr[d][sl, s * pitch:s * pitch + seg, :] = a[:, sl * LANES:(sl + 1) * LANES]
                u_scr[d][sl, s * pitch:s * pitch + seg, :] = u[:, sl * LANES:(sl + 1) * LANES]

    def rows(t):
        return pl.ds(t, N_SEG, stride=pitch)

    def scan_body(t, carry):
        new = []
        tb = seg - 1 - t
        for sl in range(nslab):
            hf, pf, hb, pb = carry[sl]
            af = a0[sl, rows(t), :]
            hf = af * hf + u0[sl, rows(t), :]
            pf = pf * af
            u0[sl, rows(t), :] = hf
            a0[sl, rows(t), :] = pf
            ab = a1[sl, rows(tb), :]
            hb = ab * hb + u1[sl, rows(tb), :]
            pb = pb * ab
            u1[sl, rows(tb), :] = hb
            a1[sl, rows(tb), :] = pb
            new.append((hf, pf, hb, pb))
        return tuple(new)

    zero = jnp.zeros((N_SEG, LANES), F32)
    one = jnp.ones((N_SEG, LANES), F32)
    fin = lax.fori_loop(0, seg, scan_body, tuple((zero, one, zero, one) for _ in range(nslab)),
                        unroll=4)

    cf, cb = [], []
    for sl in range(nslab):
        hf, pf, hb, pb = fin[sl]
        c = jnp.zeros((1, LANES), F32)
        cfr = [c]
        for s in range(1, N_SEG):
            c = pf[s - 1:s, :] * c + hf[s - 1:s, :]
            cfr.append(c)
        cf.append(jnp.concatenate(cfr, axis=0))
        c = jnp.zeros((1, LANES), F32)
        cbr = [c]
        for s in range(N_SEG - 2, -1, -1):
            c = pb[s + 1:s + 2, :] * c + hb[s + 1:s + 2, :]
            cbr.append(c)
        cb.append(jnp.concatenate(cbr[::-1], axis=0))

    def fix_body(t, carry):
        for sl in range(nslab):
            hf = u0[sl, rows(t), :] + a0[sl, rows(t), :] * cf[sl]
            hb = u1[sl, rows(t), :] + a1[sl, rows(t), :] * cb[sl]
            u0[sl, rows(t), :] = hf + hb
        return carry

    lax.fori_loop(0, seg, fix_body, 0, unroll=4)

    for s in range(N_SEG):
        g = _gelu_tanh(ga_ref[s * seg:(s + 1) * seg, :].astype(F32))
        for sl in range(nslab):
            h = u0[sl, s * pitch:s * pitch + seg, :]
            o_ref[s * seg:(s + 1) * seg, sl * LANES:(sl + 1) * LANES] = (
                h * g[:, sl * LANES:(sl + 1) * LANES]).astype(o_ref.dtype)


def _rglru(proj, cw, cb, wd, bd, lam, batch, seq):
    w = MXU_DIM
    nt = D_MODEL // w
    nslab = w // LANES
    rows = N_SEG * (seq // N_SEG + SEG_PAD)
    scan_scratch = pltpu.VMEM((nslab, rows, LANES), F32)
    return pl.pallas_call(
        functools.partial(_rglru_kernel, seq=seq),
        grid=(batch, nt),
        in_specs=[pl.BlockSpec((seq, w), lambda b, c: (b, OFF_AX // w + c)),
                  pl.BlockSpec((seq, w), lambda b, c: (b, OFF_AG // w + c)),
                  pl.BlockSpec((CONV_WIDTH, w), lambda b, c: (0, c)),
                  pl.BlockSpec((1, w), lambda b, c: (0, c)),
                  pl.BlockSpec((1, w, 4 * w), lambda b, c: (c, 0, 0)),
                  pl.BlockSpec((1, 1, 4 * w), lambda b, c: (c, 0, 0)),
                  pl.BlockSpec((2, w), lambda b, c: (0, c))],
        out_specs=pl.BlockSpec((seq, w), lambda b, c: (b, c)),
        out_shape=jax.ShapeDtypeStruct((batch * seq, D_MODEL), BF16),
        scratch_shapes=[pltpu.VMEM((seq + 16, w), F32),
                        scan_scratch, scan_scratch, scan_scratch, scan_scratch],
        compiler_params=_cparams(("parallel", "parallel")),
        name="rglru",
    )(proj, proj, cw, cb, wd, bd, lam)


def _rope_mxu(x, perm, c, s):
    return x * c + _dot(x.astype(BF16), perm) * s


def _rope_perm(width, rot, period):
    half = rot // 2
    lane = jnp.arange(width)
    j = lane % period
    src = jnp.where(j < half, lane + half, lane - half)
    sign = jnp.where(j < half, -1.0, 1.0) * (j < rot)
    return ((lane[:, None] == src[None, :]) * sign[None, :]).astype(BF16)


def _rope_cs_tables(seq, rot, period, theta):
    pos = jnp.arange(seq, dtype=F32)
    inv_freq = theta ** (-jnp.arange(0, rot, 2, dtype=F32) / rot)
    ang = pos[:, None] * inv_freq[None, :]
    cos, sin = jnp.cos(ang), jnp.sin(ang)
    pad = period - rot
    c = jnp.concatenate([cos, cos, jnp.ones((seq, pad), F32)], axis=1)
    s = jnp.concatenate([sin, sin, jnp.zeros((seq, pad), F32)], axis=1)
    rep = LANES // period
    return jnp.tile(c, (1, rep)), jnp.tile(s, (1, rep))


def _diff_prep_kernel(q_ref, k_ref, gq_ref, gk_ref, c_ref, s_ref, gm_ref, rp_ref,
                      qo_ref, ko_ref, *, qscale):
    c = jnp.concatenate([c_ref[...]] * (MXU_DIM // LANES), axis=1)
    sn = jnp.concatenate([s_ref[...]] * (MXU_DIM // LANES), axis=1)
    gm = gm_ref[...]
    rp = rp_ref[...]
    for src, g_ref, is_key in ((q_ref, gq_ref, False), (k_ref, gk_ref, True)):
        for blk in range(D_MODEL // MXU_DIM):
            x = src[:, blk * MXU_DIM:(blk + 1) * MXU_DIM].astype(F32)
            ms = _dot((x * x).astype(BF16), gm)
            xr = _rope_mxu(x * lax.rsqrt(ms + EPS) * g_ref[...], rp, c, sn)
            for h in range(MXU_DIM // LANES):
                xb = xr[:, h * LANES:(h + 1) * LANES]
                lo = blk * MXU_DIM + h * LANES
                if is_key:
                    ko_ref[lo:lo + LANES, :] = xb.T.astype(ko_ref.dtype)
                else:
                    qo_ref[:, lo:lo + LANES] = (xb * qscale).astype(qo_ref.dtype)


def _diff_prep(proj, gq, gk, tabs, gm, rp, seq, tm, qscale):
    m = proj.shape[0]
    nrow = seq // tm
    tab_spec = pl.BlockSpec((tm, LANES), lambda i: (i % nrow, 0))
    return pl.pallas_call(
        functools.partial(_diff_prep_kernel, qscale=qscale),
        grid=(m // tm,),
        in_specs=[pl.BlockSpec((tm, D_MODEL), lambda i: (i, OFF_BQ // D_MODEL)),
                  pl.BlockSpec((tm, D_MODEL), lambda i: (i, OFF_BK // D_MODEL)),
                  pl.BlockSpec((1, MXU_DIM), lambda i: (0, 0)),
                  pl.BlockSpec((1, MXU_DIM), lambda i: (0, 0)),
                  tab_spec, tab_spec,
                  pl.BlockSpec((MXU_DIM, MXU_DIM), lambda i: (0, 0)),
                  pl.BlockSpec((MXU_DIM, MXU_DIM), lambda i: (0, 0))],
        out_specs=[pl.BlockSpec((tm, D_MODEL), lambda i: (i, 0)),
                   pl.BlockSpec((None, D_MODEL, tm), lambda i: (i // nrow, 0, i % nrow))],
        out_shape=[jax.ShapeDtypeStruct((m, D_MODEL), BF16),
                   jax.ShapeDtypeStruct((m // seq, D_MODEL, seq), BF16)],
        compiler_params=_cparams(("parallel",)),
        name="diff_prep",
    )(proj, proj, gq, gk, *tabs, gm, rp)


def _mla_prep_kernel(c_ref, qag_ref, wuq_ref, kvag_ref, wukv_ref, gq_ref, gk_ref,
                     c_tab, s_tab, rp_ref, qo_ref, ko_ref, vo_ref, *, qscale):
    dh = MLA_NOPE + MLA_ROPE
    ct, st, rp = c_tab[...], s_tab[...], rp_ref[...]

    def sumsq(x):
        return _dot((x * x).astype(BF16), jnp.ones((x.shape[1], LANES), BF16))

    def rms_wide(x, g):
        inv = lax.rsqrt(sumsq(x) * (1.0 / x.shape[1]) + EPS)
        return x * jnp.concatenate([inv] * (x.shape[1] // LANES), axis=1) * g

    cq = c_ref[:, 0:MLA_Q_LORA].astype(F32)
    ckv = c_ref[:, MLA_Q_LORA:MLA_Q_LORA + MLA_KV_LORA].astype(F32)
    kpe = c_ref[:, MLA_Q_LORA + MLA_KV_LORA:C_WIDTH].astype(F32)
    q = _dot(rms_wide(cq, qag_ref[...]).astype(BF16), wuq_ref[...])
    kv = _dot(rms_wide(ckv, kvag_ref[...]).astype(BF16), wukv_ref[...])
    gq_n, gq_r = gq_ref[:, 0:LANES], gq_ref[:, LANES:2 * LANES]
    gk_n, gk_r = gk_ref[:, 0:LANES], gk_ref[:, LANES:2 * LANES]
    kpe_ss = sumsq(kpe)
    for h in range(MLA_HEADS):
        lo = h * MLA_HEAD_PAD
        qn, qr = q[:, lo:lo + LANES], q[:, lo + LANES:lo + 2 * LANES]
        qh = q[:, lo:lo + 2 * LANES]
        inv = lax.rsqrt(jnp.sum(qh * qh, axis=-1, keepdims=True) * (1.0 / dh) + EPS) * qscale
        qo_ref[:, lo:lo + LANES] = (qn * inv * gq_n).astype(qo_ref.dtype)
        qo_ref[:, lo + LANES:lo + 2 * LANES] = _rope_mxu(
            qr * inv * gq_r, rp, ct, st).astype(qo_ref.dtype)
        kn = kv[:, h * LANES:(h + 1) * LANES]
        inv = lax.rsqrt((jnp.sum(kn * kn, axis=-1, keepdims=True) + kpe_ss) * (1.0 / dh) + EPS)
        ko_ref[lo:lo + LANES, :] = (kn * inv * gk_n).T.astype(ko_ref.dtype)
        ko_ref[lo + LANES:lo + 2 * LANES, :] = _rope_mxu(
            kpe * inv * gk_r, rp, ct, st).T.astype(ko_ref.dtype)
    vo_ref[...] = kv[:, MLA_HEADS * LANES:].astype(vo_ref.dtype)


def _mla_prep(proj, qag, wuq, kvag, wukv, gq, gk, tabs, rp, seq, tm, qscale):
    m = proj.shape[0]
    nrow = seq // tm
    hp = MLA_HEADS * MLA_HEAD_PAD
    tab_spec = pl.BlockSpec((tm, LANES), lambda i: (i % nrow, 0))
    const = lambda shape: pl.BlockSpec(shape, lambda i: (0, 0))
    return pl.pallas_call(
        functools.partial(_mla_prep_kernel, qscale=qscale),
        grid=(m // tm,),
        in_specs=[pl.BlockSpec((tm, C_WIDTH), lambda i: (i, OFF_C // C_WIDTH)),
                  const((1, MLA_Q_LORA)), const((MLA_Q_LORA, hp)),
                  const((1, MLA_KV_LORA)), const((MLA_KV_LORA, 2 * MLA_HEADS * LANES)),
                  const((1, MLA_HEAD_PAD)), const((1, MLA_HEAD_PAD)),
                  tab_spec, tab_spec, const((LANES, LANES))],
        out_specs=[pl.BlockSpec((tm, hp), lambda i: (i, 0)),
                   pl.BlockSpec((None, hp, tm), lambda i: (i // nrow, 0, i % nrow)),
                   pl.BlockSpec((tm, MLA_HEADS * MLA_V), lambda i: (i, 0))],
        out_shape=[jax.ShapeDtypeStruct((m, hp), BF16),
                   jax.ShapeDtypeStruct((m // seq, hp, seq), BF16),
                   jax.ShapeDtypeStruct((m, MLA_HEADS * MLA_V), BF16)],
        compiler_params=_cparams(("parallel",)),
        name="mla_prep",
    )(proj, qag, wuq, kvag, wukv, gq, gk, *tabs, rp)


def _attn_kernel(q_ref, k_ref, v_ref, lam_ref, sg_ref, o_ref, vext, m_scr, acc_scr,
                 *, seq, tk, diff, out_scale):
    tq = q_ref.shape[0]

    @pl.when(pl.program_id(2) == 0)
    def _():
        vext[:, 0:LANES] = v_ref[...]
        vext[:, LANES:2 * LANES] = jnp.ones((seq, LANES), BF16)

    q = q_ref[...]
    if diff:
        lane = lax.broadcasted_iota(jnp.int32, q.shape, 1)
        zero = jnp.zeros_like(q)
        q = jnp.concatenate([jnp.where(lane < DIFF_DH, q, zero),
                             jnp.where(lane >= DIFF_DH, q, zero)], axis=0)
    rows = q.shape[0]
    m_scr[...] = jnp.full((rows, LANES), -1e30, F32)
    acc_scr[...] = jnp.zeros((rows, 2 * LANES), F32)

    def body(j, carry):
        off = pl.multiple_of(j * tk, tk)
        s = _dot(q, k_ref[:, pl.ds(off, tk)])
        m_prev = m_scr[...]
        m_new = jnp.maximum(m_prev, jnp.max(s, axis=1, keepdims=True))
        alpha = jnp.exp2(m_prev - m_new)
        p = jnp.exp2(s - m_new[:, 0:1]).astype(BF16)
        acc_scr[...] = (acc_scr[...] * jnp.concatenate([alpha, alpha], axis=1)
                        + _dot(p, vext[pl.ds(off, tk), :]))
        m_scr[...] = m_new
        return carry

    lax.fori_loop(0, seq // tk, body, 0)

    acc = acc_scr[...]
    o = acc[:, 0:LANES] / acc[:, LANES:2 * LANES]
    if diff:
        o = o[0:tq, :] - lam_ref[...] * o[tq:2 * tq, :]
        o = _rms(o, sg_ref[...]) * out_scale
    o_ref[...] = o.astype(o_ref.dtype)


def _attn_bounded_kernel(q_ref, k_ref, v_ref, lam_ref, sg_ref, o_ref, vext,
                         *, seq, tk, diff, out_scale):
    tq = q_ref.shape[0]

    @pl.when(pl.program_id(2) == 0)
    def _():
        vext[:, 0:LANES] = v_ref[...]
        vext[:, LANES:2 * LANES] = jnp.ones((seq, LANES), BF16)

    q = q_ref[...]
    if diff:
        lane = lax.broadcasted_iota(jnp.int32, q.shape, 1)
        zero = jnp.zeros_like(q)
        q = jnp.concatenate([jnp.where(lane < DIFF_DH, q, zero),
                             jnp.where(lane >= DIFF_DH, q, zero)], axis=0)
    rb = ATTN_ROW_BLOCK

    def softmax_rows(lo):
        acc = None
        for j in range(seq // tk):
            p = jnp.exp2(_dot(q[lo:lo + rb, :], k_ref[:, j * tk:(j + 1) * tk])).astype(BF16)
            pv = _dot(p, vext[j * tk:(j + 1) * tk, :])
            acc = pv if acc is None else acc + pv
        return acc[:, 0:LANES] / acc[:, LANES:2 * LANES]

    for r in range(tq // rb):
        o = softmax_rows(r * rb)
        if diff:
            o = o - lam_ref[...] * softmax_rows(tq + r * rb)
            o = _rms(o, sg_ref[...]) * out_scale
        o_ref[r * rb:(r + 1) * rb, :] = o.astype(o_ref.dtype)


def _score_bound(gq, gk, dh):
    return 1.02 * LOG2E * math.sqrt(dh) * jnp.max(jnp.abs(gq)) * jnp.max(jnp.abs(gk))


def _attention(q, k, v, lam, sg, *, v_off, batch, seq, heads, dq, tq, tk, diff, out_scale, bounded):
    nq = seq // tq
    rows = 2 * tq if diff else tq
    vblk = v_off // LANES
    body = _attn_bounded_kernel if bounded else _attn_kernel
    return pl.pallas_call(
        functools.partial(body, seq=seq, tk=tk, diff=diff, out_scale=out_scale),
        grid=(batch, heads, nq),
        in_specs=[pl.BlockSpec((tq, dq), lambda b, h, i: (b * nq + i, h)),
                  pl.BlockSpec((None, dq, seq), lambda b, h, i: (b, h, 0)),
                  pl.BlockSpec((seq, LANES), lambda b, h, i: (b, vblk + h)),
                  pl.BlockSpec((1, LANES), lambda b, h, i: (0, 0)),
                  pl.BlockSpec((1, LANES), lambda b, h, i: (0, 0))],
        out_specs=pl.BlockSpec((tq, LANES), lambda b, h, i: (b * nq + i, h)),
        out_shape=jax.ShapeDtypeStruct((batch * seq, heads * LANES), BF16),
        scratch_shapes=[pltpu.VMEM((seq, 2 * LANES), BF16)] + ([] if bounded else [
            pltpu.VMEM((rows, LANES), F32), pltpu.VMEM((rows, 2 * LANES), F32)]),
        compiler_params=_cparams(("parallel", "parallel", "arbitrary")),
        name=("diff_attn" if diff else "mla_attn") + ("_bounded" if bounded else "_online"),
    )(q, k, v, lam, sg)


def _retention_kernel(q_ref, k_ref, v_ref, g_ref, lg_ref, gn_ref, c_tab, s_tab, rp_ref,
                      o_ref, qs, kts, o_fwd, o_bwd, state, *, seq):
    ch = RET_CHUNK if seq % RET_CHUNK == 0 else seq
    nc = seq // ch
    ct, st, rp = c_tab[...], s_tab[...], rp_ref[...]
    qs[...] = _rope_mxu(q_ref[...].astype(F32), rp, ct, st)
    kt = (_rope_mxu(k_ref[...].astype(F32), rp, ct, st) * (RET_DK ** -0.5)).T
    row = lax.broadcasted_iota(jnp.int32, kt.shape, 0)
    kts[0] = jnp.where(row < RET_DK, kt, 0.0)
    kts[1] = jnp.where(row >= RET_DK, kt, 0.0)
    state[...] = jnp.zeros(state.shape, F32)

    ri = lax.broadcasted_iota(jnp.int32, (ch, ch), 0).astype(F32)
    ci = lax.broadcasted_iota(jnp.int32, (ch, ch), 1).astype(F32)
    qidx = lax.broadcasted_iota(jnp.int32, (ch, 1), 0).astype(F32)
    kidx = lax.broadcasted_iota(jnp.int32, (1, ch), 1).astype(F32)
    dec = []
    for hh in range(2):
        lg = lg_ref[hh:hh + 1, 0:1]
        dec.append(dict(
            intra=jnp.exp(lg * jnp.abs(ri - ci)),
            tail=jnp.exp(lg * (ch - 1.0 - kidx)),
            head=jnp.exp(lg * kidx),
            qf=jnp.exp(lg * (qidx + 1.0)),
            qb=jnp.exp(lg * (ch - qidx)),
            chunk=jnp.exp(lg * float(ch))))

    def sweep(t, carry):
        of = pl.multiple_of(t * ch, ch)
        ob = pl.multiple_of((nc - 1 - t) * ch, ch)
        qf = qs[pl.ds(of, ch), :]
        qb = qs[pl.ds(ob, ch), :]
        for hh in range(2):
            d = dec[hh]
            vlo = hh * RET_DV
            ktf = kts[hh, :, pl.ds(of, ch)]
            vf = v_ref[pl.ds(of, ch), vlo:vlo + RET_DV]
            sc = _dot(qf.astype(BF16), ktf.astype(BF16)) * d["intra"]
            past = state[2 * hh]
            o_fwd[hh, pl.ds(of, ch), :] = (_dot(sc.astype(BF16), vf)
                                           + _dot((qf * d["qf"]).astype(BF16), past.astype(BF16)))
            state[2 * hh] = past * d["chunk"] + _dot((ktf * d["tail"]).astype(BF16), vf)
            ktb = kts[hh, :, pl.ds(ob, ch)]
            vb = v_ref[pl.ds(ob, ch), vlo:vlo + RET_DV]
            fut = state[2 * hh + 1]
            o_bwd[hh, pl.ds(ob, ch), :] = _dot((qb * d["qb"]).astype(BF16), fut.astype(BF16))
            state[2 * hh + 1] = fut * d["chunk"] + _dot((ktb * d["head"]).astype(BF16), vb)
        return carry

    lax.fori_loop(0, nc, sweep, 0, unroll=2 if nc % 2 == 0 else 1)

    lane_mean = jnp.full((RET_DV, RET_DV), 1.0 / RET_DV, BF16)

    def fin(n, carry):
        off = pl.multiple_of(n * ch, ch)
        for hh in range(2):
            vlo = hh * RET_DV
            o = o_fwd[hh, pl.ds(off, ch), :] + o_bwd[hh, pl.ds(off, ch), :]
            oc = o - _dot(o.astype(BF16), lane_mean)
            var = _dot((oc * oc).astype(BF16), lane_mean)
            y = oc * lax.rsqrt(var + EPS) * gn_ref[...]
            g = g_ref[pl.ds(off, ch), vlo:vlo + RET_DV].astype(F32)
            o_ref[pl.ds(off, ch), vlo:vlo + RET_DV] = (g * _sigmoid(g) * y).astype(o_ref.dtype)
        return carry

    lax.fori_loop(0, nc, fin, 0, unroll=2 if nc % 2 == 0 else 1)


def _retention(proj, lg, gn, tabs, rp, batch, seq):
    npair = RET_HEADS // 2
    pw = 2 * RET_DV
    tab_spec = pl.BlockSpec((seq, LANES), lambda b, j: (0, 0))
    return pl.pallas_call(
        functools.partial(_retention_kernel, seq=seq),
        grid=(batch, npair),
        in_specs=[pl.BlockSpec((seq, LANES), lambda b, j: (b, OFF_DQ // LANES + j)),
                  pl.BlockSpec((seq, LANES), lambda b, j: (b, OFF_DK // LANES + j)),
                  pl.BlockSpec((seq, pw), lambda b, j: (b, OFF_DV // pw + j)),
                  pl.BlockSpec((seq, pw), lambda b, j: (b, OFF_DG // pw + j)),
                  pl.BlockSpec((None, 2, LANES), lambda b, j: (j, 0, 0)),
                  pl.BlockSpec((1, RET_DV), lambda b, j: (0, 0)),
                  tab_spec, tab_spec, pl.BlockSpec((LANES, LANES), lambda b, j: (0, 0))],
        out_specs=pl.BlockSpec((seq, pw), lambda b, j: (b, j)),
        out_shape=jax.ShapeDtypeStruct((batch * seq, RET_HEADS * RET_DV), BF16),
        scratch_shapes=[pltpu.VMEM((seq, LANES), F32),
                        pltpu.VMEM((2, LANES, seq), F32),
                        pltpu.VMEM((2, seq, RET_DV), F32),
                        pltpu.VMEM((2, seq, RET_DV), F32),
                        pltpu.VMEM((4, LANES, RET_DV), F32)],
        compiler_params=_cparams(("parallel", "parallel")),
        name="retention",
    )(proj, proj, proj, proj, lg, gn, *tabs, rp)


def _merge_kernel(x_ref, gl_ref, gb_ref, ya_ref, yb_ref, yc_ref, yd_ref,
                  wa_ref, wb_ref, wc_ref, wd_ref, wo_ref, o_ref):
    for r in range(x_ref.shape[0] // NORM_ROW_BLOCK):
        rows = slice(r * NORM_ROW_BLOCK, (r + 1) * NORM_ROW_BLOCK)
        merged = None
        for n, (y_ref, w_ref) in enumerate(((ya_ref, wa_ref), (yb_ref, wb_ref),
                                            (yc_ref, wc_ref), (yd_ref, wd_ref))):
            lo = n * D_MODEL
            gate = _sigmoid(gl_ref[rows, lo:lo + D_MODEL].astype(F32)
                            + gb_ref[:, lo:lo + D_MODEL])
            term = gate * _dot(y_ref[rows, :], w_ref[...])
            merged = term if merged is None else merged + term
        o_ref[rows, :] = x_ref[rows, :] + _dot(merged.astype(BF16), wo_ref[...])


def _merge(x2, proj, gb, ys, ws, wo, tm):
    m = x2.shape[0]
    row = lambda width: pl.BlockSpec((tm, width), lambda i: (i, 0))
    wspec = pl.BlockSpec((D_MODEL, D_MODEL), lambda i: (0, 0), pipeline_mode=pl.Buffered(1))
    return pl.pallas_call(
        _merge_kernel,
        grid=(m // tm,),
        in_specs=[row(D_MODEL), row(N_BRANCH * D_MODEL),
                  pl.BlockSpec((1, N_BRANCH * D_MODEL), lambda i: (0, 0)),
                  row(D_MODEL), row(D_MODEL), row(D_MODEL), row(D_MODEL),
                  wspec, wspec, wspec, wspec, wspec],
        out_specs=row(D_MODEL),
        out_shape=jax.ShapeDtypeStruct((m, D_MODEL), F32),
        compiler_params=_cparams(("parallel",)),
        name="merge",
    )(x2, proj, gb, *ys, *ws, wo)


def _ffn_kernel(x_ref, g2_ref, w1_ref, w2_ref, g3_ref, wpg_ref, p_ref, wpe_ref, o_ref,
                acc_ref):
    j = pl.program_id(1)
    blocks = [slice(r * NORM_ROW_BLOCK, (r + 1) * NORM_ROW_BLOCK)
              for r in range(x_ref.shape[0] // NORM_ROW_BLOCK)]

    @pl.when(j == 0)
    def _():
        acc_ref[...] = jnp.zeros_like(acc_ref)

    for rows in blocks:
        xn = _rms(x_ref[rows, :], g2_ref[...]).astype(BF16)
        h = jnp.maximum(_dot(xn, w1_ref[...]), 0.0)
        acc_ref[rows, :] += _dot((h * h).astype(BF16), w2_ref[...])

    @pl.when(j == pl.num_programs(1) - 1)
    def _():
        for rows in blocks:
            x = x_ref[rows, :] + acc_ref[rows, :]
            gate = _sigmoid(_dot(_rms(x, g3_ref[...]).astype(BF16), wpg_ref[...]))
            o_ref[rows, :] = x + gate * _dot(p_ref[rows, :].astype(BF16), wpe_ref[...])


def _ffn(x2, g2, w1, w2, g3, wpg, p2, layer, wpe, tm, tf):
    m = x2.shape[0]
    dff = w1.shape[1]
    pdim = p2.shape[1]
    p_row0 = layer * (m // tm)
    return pl.pallas_call(
        _ffn_kernel,
        grid=(m // tm, dff // tf),
        in_specs=[pl.BlockSpec((tm, D_MODEL), lambda i, j: (i, 0)),
                  pl.BlockSpec((1, D_MODEL), lambda i, j: (0, 0)),
                  pl.BlockSpec((D_MODEL, tf), lambda i, j: (0, j)),
                  pl.BlockSpec((tf, D_MODEL), lambda i, j: (j, 0)),
                  pl.BlockSpec((1, D_MODEL), lambda i, j: (0, 0)),
                  pl.BlockSpec((D_MODEL, D_MODEL), lambda i, j: (0, 0)),
                  pl.BlockSpec((tm, pdim), lambda i, j: (p_row0 + i, 0)),
                  pl.BlockSpec((pdim, D_MODEL), lambda i, j: (0, 0))],
        out_specs=pl.BlockSpec((tm, D_MODEL), lambda i, j: (i, 0)),
        out_shape=jax.ShapeDtypeStruct((m, D_MODEL), F32),
        scratch_shapes=[pltpu.VMEM((tm, D_MODEL), F32)],
        compiler_params=_cparams(("parallel", "arbitrary")),
        name="ffn_ple",
    )(x2, g2, w1, w2, g3, wpg, p2, wpe)


def _block_diag(blocks):
    n, c, d = blocks.shape
    eye = jnp.eye(n, dtype=blocks.dtype)
    return jnp.einsum('ncd,nm->ncmd', blocks, eye).reshape(n * c, n * d)


def _permute_w_in(w_in):
    sizes = (1024, 1024, 1024, 1024, 1024, 384, 256, 64, 512, 512, 1024, 1024, 4096)
    offs = [0]
    for s in sizes:
        offs.append(offs[-1] + s)
    (a_x, a_g, b_q, b_k, b_v, c_q, c_kv, c_kpe, d_q, d_k, d_v, d_g, gates) = [
        w_in[:, offs[n]:offs[n + 1]] for n in range(len(sizes))]
    pad = jnp.zeros((w_in.shape[0], C_WIDTH - 704), w_in.dtype)
    return jnp.concatenate([gates, a_x, a_g, b_q, b_k, b_v, d_v, d_g, d_q, d_k,
                            c_q, c_kv, c_kpe, pad], axis=1).astype(BF16)


def _lru_weights(wa, ba, wx, bx):
    w = MXU_DIM
    nt = D_MODEL // w
    dense = [_block_diag(m) for m in (wa[0], wx[0], wa[1], wx[1])]
    wd = jnp.stack([jnp.concatenate([d[c * w:(c + 1) * w, c * w:(c + 1) * w] for d in dense], axis=1)
                    for c in range(nt)]).astype(BF16)
    bias = (ba[0], bx[0], ba[1], bx[1])
    bd = jnp.stack([jnp.concatenate([b[c * w:(c + 1) * w] for b in bias])[None, :]
                    for c in range(nt)])
    return wd, bd


def _mla_weights(wuq, wukv, gq, gk):
    dh = MLA_NOPE + MLA_ROPE
    zpad = MLA_HEAD_PAD - dh
    wq = wuq.reshape(MLA_Q_LORA, MLA_HEADS, dh)
    wq = jnp.pad(wq, ((0, 0), (0, 0), (0, zpad))).reshape(MLA_Q_LORA, MLA_HEADS * MLA_HEAD_PAD)
    wkv = wukv.reshape(MLA_KV_LORA, MLA_HEADS, MLA_NOPE + MLA_V)
    wk = wkv[:, :, :MLA_NOPE].reshape(MLA_KV_LORA, MLA_HEADS * MLA_NOPE)
    wv = wkv[:, :, MLA_NOPE:].reshape(MLA_KV_LORA, MLA_HEADS * MLA_V)
    gqp = jnp.pad(gq, (0, zpad))[None, :]
    gkp = jnp.pad(gk, (0, zpad))[None, :]
    return wq.astype(BF16), jnp.concatenate([wk, wv], axis=1).astype(BF16), gqp, gkp


def kernel(x, p, norm1_g, w_in, gate_b, conv_w, conv_b, lru_wa, lru_ba, lru_wx, lru_bx, lru_lambda,
           diff_q_g, diff_k_g, diff_lam, diff_sub_g, mla_qa_g, mla_wuq, mla_kva_g, mla_wukv,
           mla_q_g, mla_k_g, ret_gn_g, w_br_a, w_br_b, w_br_c, w_br_d, w_out, norm2_g,
           w_ff1, w_ff2, norm3_g, w_ple_gate, w_ple_proj):
    batch, seq, d_model = x.shape
    depth = w_in.shape[0]
    assert d_model == D_MODEL and seq % (8 * N_SEG) == 0
    m = batch * seq
    tm_big = min(1024, seq)
    tm_small = min(512, seq)
    tq = min(2048, seq)
    tk = min(512, seq)

    tabs_b = _rope_cs_tables(seq, DIFF_ROT, DIFF_DH, ROPE_THETA)
    tabs_c = _rope_cs_tables(seq, MLA_ROPE, LANES, ROPE_THETA)
    rp_b = _rope_perm(MXU_DIM, DIFF_ROT, DIFF_DH)
    rp_c = _rope_perm(LANES, MLA_ROPE, LANES)
    tabs_d = _rope_cs_tables(seq, RET_DK, RET_DK, RET_THETA)
    rp_d = _rope_perm(LANES, RET_DK, RET_DK)
    log_g = jnp.log1p(-jnp.exp2(-5.0 - jnp.arange(RET_HEADS, dtype=F32)))
    lg = jnp.broadcast_to(log_g.reshape(RET_HEADS // 2, 2, 1), (RET_HEADS // 2, 2, LANES))
    grp = jnp.arange(MXU_DIM) // DIFF_DH
    gm = ((grp[:, None] == grp[None, :]).astype(F32) / DIFF_DH).astype(BF16)
    ones_row = jnp.ones((1, LANES), F32)

    x2 = x.reshape(m, d_model)
    for i in range(depth):
        proj = _inproj(x2, norm1_g[i][None, :], _permute_w_in(w_in[i]), tm_big, N_PROJ // 3)

        wd, bd = _lru_weights(lru_wa[i], lru_ba[i], lru_wx[i], lru_bx[i])
        y_a = _rglru(proj, conv_w[i], conv_b[i][None, :], wd, bd, lru_lambda[i], batch, seq)

        lam_init = 0.8 - 0.6 * math.exp(-0.3 * i)
        lp = diff_lam[i].astype(F32)
        lam = jnp.exp(jnp.sum(lp[0] * lp[1])) - jnp.exp(jnp.sum(lp[2] * lp[3])) + lam_init
        gq = jnp.tile(diff_q_g[i], MXU_DIM // DIFF_DH)[None, :]
        gk = jnp.tile(diff_k_g[i], MXU_DIM // DIFF_DH)[None, :]
        qb, kb = _diff_prep(proj, gq, gk, tabs_b, gm, rp_b, seq, tm_small, DIFF_DH ** -0.5 * LOG2E)
        attn_b = functools.partial(_attention, v_off=OFF_BV, batch=batch, seq=seq, heads=DIFF_HEADS,
                                   dq=LANES, tq=tq, tk=tk, diff=True, out_scale=1.0 - lam_init)
        y_b = lax.cond(
            _score_bound(diff_q_g[i], diff_k_g[i], DIFF_DH) <= MAX_SCORE_LOG2,
            functools.partial(attn_b, bounded=True), functools.partial(attn_b, bounded=False),
            qb, kb, proj, jnp.full((1, LANES), lam, F32), diff_sub_g[i][None, :])

        wq, wkv, gqp, gkp = _mla_weights(mla_wuq[i], mla_wukv[i], mla_q_g[i], mla_k_g[i])
        qc, kc, vc = _mla_prep(proj, mla_qa_g[i][None, :], wq, mla_kva_g[i][None, :], wkv,
                               gqp, gkp, tabs_c, rp_c, seq, tm_small,
                               (MLA_NOPE + MLA_ROPE) ** -0.5 * LOG2E)
        attn_c = functools.partial(_attention, v_off=0, batch=batch, seq=seq, heads=MLA_HEADS,
                                   dq=MLA_HEAD_PAD, tq=min(2 * tq, seq), tk=tk, diff=False,
                                   out_scale=1.0)
        y_c = lax.cond(
            _score_bound(mla_q_g[i], mla_k_g[i], MLA_NOPE + MLA_ROPE) <= MAX_SCORE_LOG2,
            functools.partial(attn_c, bounded=True), functools.partial(attn_c, bounded=False),
            qc, kc, vc, ones_row, ones_row)

        y_d = _retention(proj, lg, ret_gn_g[i][None, :], tabs_d, rp_d, batch, seq)

        x2 = _merge(x2, proj, gate_b[i].reshape(1, N_BRANCH * D_MODEL), (y_a, y_b, y_c, y_d),
                    tuple(w[i].astype(BF16) for w in (w_br_a, w_br_b, w_br_c, w_br_d)),
                    w_out[i].astype(BF16), tm_small)

        x2 = _ffn(x2, norm2_g[i][None, :], w_ff1[i].astype(BF16), w_ff2[i].astype(BF16),
                  norm3_g[i][None, :], w_ple_gate[i].astype(BF16),
                  p.reshape(depth * m, p.shape[-1]), i, w_ple_proj[i].astype(BF16), tm_big, 2048)
    return x2.reshape(batch, seq, d_model)
```

```python
import functools
import math

import jax
import jax.numpy as jnp
from jax import lax
from jax.experimental import pallas as pl
from jax.experimental.pallas import tpu as pltpu

F32 = jnp.float32
BF16 = jnp.bfloat16

EPS = 1e-6
ROPE_THETA = 500000.0
RET_THETA = 10000.0
LRU_C = 8.0
LOG2E = 1.4426950408889634

LANES = 128
MXU_DIM = 256
VMEM_LIMIT = 56 * 1024 * 1024

D_MODEL = 1024
N_BRANCH = 4
LRU_BLOCKS = 16
CONV_WIDTH = 4
DIFF_HEADS = 8
DIFF_DH = 64
DIFF_ROT = 16
MLA_HEADS = 8
MLA_NOPE = 128
MLA_ROPE = 64
MLA_V = 128
MLA_Q_LORA = 384
MLA_KV_LORA = 256
MLA_HEAD_PAD = 256
RET_HEADS = 8
RET_DK = 64
RET_DV = 128
RET_CHUNK = 256
ATTN_ROW_BLOCK = 128
NORM_ROW_BLOCK = 256
MAX_SCORE_LOG2 = 60.0
N_SEG = 8
SEG_PAD = 4

OFF_GATE = 0
OFF_A = 4096
A_TILE = 2 * MXU_DIM
OFF_BQ = 6144
OFF_BK = 7168
OFF_BV = 8192
OFF_D = 9216
D_PAIR = 4 * RET_DK + 4 * RET_DV
OFF_C = 12288
C_WIDTH = 768
N_PROJ = OFF_C + C_WIDTH


def _cparams(sem):
    return pltpu.CompilerParams(dimension_semantics=sem, vmem_limit_bytes=VMEM_LIMIT)


def _sigmoid(x):
    return 1.0 / (1.0 + jnp.exp2(x * -LOG2E))


def _rms(x, g):
    return x * lax.rsqrt(jnp.mean(x * x, axis=-1, keepdims=True) + EPS) * g


def _dot(a, b):
    return jnp.dot(a, b, preferred_element_type=F32)


def _dot_nt(a, b):
    return lax.dot_general(a, b, (((1,), (1,)), ((), ())), preferred_element_type=F32)


def _dot_tn(a, b):
    return lax.dot_general(a, b, (((0,), (0,)), ((), ())), preferred_element_type=F32)


def _inproj_kernel(x_ref, g_ref, w_ref, o_ref):
    for r in range(x_ref.shape[0] // NORM_ROW_BLOCK):
        rows = slice(r * NORM_ROW_BLOCK, (r + 1) * NORM_ROW_BLOCK)
        xn = _rms(x_ref[rows, :], g_ref[...]).astype(BF16)
        o_ref[rows, :] = _dot(xn, w_ref[...]).astype(o_ref.dtype)


def _inproj(x2, g, w, tm, tn):
    m, k = x2.shape
    n = w.shape[1]
    return pl.pallas_call(
        _inproj_kernel,
        grid=(m // tm, n // tn),
        in_specs=[pl.BlockSpec((tm, k), lambda i, j: (i, 0)),
                  pl.BlockSpec((1, k), lambda i, j: (0, 0)),
                  pl.BlockSpec((k, tn), lambda i, j: (0, j))],
        out_specs=pl.BlockSpec((tm, tn), lambda i, j: (i, j)),
        out_shape=jax.ShapeDtypeStruct((m, n), BF16),
        compiler_params=_cparams(("parallel", "arbitrary")),
        name="inproj",
    )(x2, g, w)


def _gelu_tanh(x):
    c = 0.7978845608028654
    return (0.5 * x) * (1.0 + jnp.tanh(x * (c + (c * 0.044715) * (x * x))))


def _rglru_kernel(xg_ref, cw_ref, cb_ref, wd_ref, bd_ref, lam_ref, o_ref,
                  xpad, a0, u0, a1, u1, *, seq):
    w = MXU_DIM
    nslab = w // LANES
    seg = seq // N_SEG
    pitch = seg + SEG_PAD
    left = CONV_WIDTH // 2

    xpad[0:8, :] = jnp.zeros((8, w), F32)
    xpad[seq + 8:seq + 16, :] = jnp.zeros((8, w), F32)
    for s in range(N_SEG):
        xpad[8 + s * seg:8 + (s + 1) * seg, :] = xg_ref[s * seg:(s + 1) * seg, 0:w].astype(F32)

    lam = lam_ref[...]
    nlam = -lam
    nls = (-LRU_C * LOG2E) * (jnp.maximum(nlam, 0.0) + jnp.log(1.0 + jnp.exp(-jnp.abs(nlam))))

    a_scr = (a0, a1)
    u_scr = (u0, u1)
    for s in range(N_SEG):
        base = 8 + s * seg - left
        xc = cb_ref[...]
        for t in range(CONV_WIDTH):
            xc = xc + xpad[base + t:base + t + seg, :] * cw_ref[t:t + 1, :]
        z = _dot(xc.astype(BF16), wd_ref[0]) + bd_ref[0]
        for d in range(2):
            r = _sigmoid(z[:, (2 * d) * w:(2 * d + 1) * w])
            gi = _sigmoid(z[:, (2 * d + 1) * w:(2 * d + 2) * w])
            a = jnp.exp2(nls[d:d + 1, :] * r)
            v = 1.0 - a * a
            u = (v * lax.rsqrt(jnp.maximum(v, 1e-30))) * (gi * xc)
            for sl in range(nslab):
                a_scr[d][sl, s * pitch:s * pitch + seg, :] = a[:, sl * LANES:(sl + 1) * LANES]
                u_scr[d][sl, s * pitch:s * pitch + seg, :] = u[:, sl * LANES:(sl + 1) * LANES]

    def rows(t):
        return pl.ds(t, N_SEG, stride=pitch)

    def scan_body(t, carry):
        new = []
        tb = seg - 1 - t
        for sl in range(nslab):
            hf, pf, hb, pb = carry[sl]
            af = a0[sl, rows(t), :]
            hf = af * hf + u0[sl, rows(t), :]
            pf = pf * af
            u0[sl, rows(t), :] = hf
            a0[sl, rows(t), :] = pf
            ab = a1[sl, rows(tb), :]
            hb = ab * hb + u1[sl, rows(tb), :]
            pb = pb * ab
            u1[sl, rows(tb), :] = hb
            a1[sl, rows(tb), :] = pb
            new.append((hf, pf, hb, pb))
        return tuple(new)

    zero = jnp.zeros((N_SEG, LANES), F32)
    one = jnp.ones((N_SEG, LANES), F32)
    fin = lax.fori_loop(0, seg, scan_body, tuple((zero, one, zero, one) for _ in range(nslab)),
                        unroll=4)

    cf, cb = [], []
    for sl in range(nslab):
        hf, pf, hb, pb = fin[sl]
        c = jnp.zeros((1, LANES), F32)
        cfr = [c]
        for s in range(1, N_SEG):
            c = pf[s - 1:s, :] * c + hf[s - 1:s, :]
            cfr.append(c)
        cf.append(jnp.concatenate(cfr, axis=0))
        c = jnp.zeros((1, LANES), F32)
        cbr = [c]
        for s in range(N_SEG - 2, -1, -1):
            c = pb[s + 1:s + 2, :] * c + hb[s + 1:s + 2, :]
            cbr.append(c)
        cb.append(jnp.concatenate(cbr[::-1], axis=0))

    def fix_body(t, carry):
        for sl in range(nslab):
            hf = u0[sl, rows(t), :] + a0[sl, rows(t), :] * cf[sl]
            hb = u1[sl, rows(t), :] + a1[sl, rows(t), :] * cb[sl]
            u0[sl, rows(t), :] = hf + hb
        return carry

    lax.fori_loop(0, seg, fix_body, 0, unroll=4)

    for s in range(N_SEG):
        g = _gelu_tanh(xg_ref[s * seg:(s + 1) * seg, w:2 * w].astype(F32))
        for sl in range(nslab):
            h = u0[sl, s * pitch:s * pitch + seg, :]
            o_ref[s * seg:(s + 1) * seg, sl * LANES:(sl + 1) * LANES] = (
                h * g[:, sl * LANES:(sl + 1) * LANES]).astype(o_ref.dtype)


def _rglru(proj, cw, cb, wd, bd, lam, batch, seq):
    w = MXU_DIM
    nt = D_MODEL // w
    nslab = w // LANES
    rows = N_SEG * (seq // N_SEG + SEG_PAD)
    scan_scratch = pltpu.VMEM((nslab, rows, LANES), F32)
    return pl.pallas_call(
        functools.partial(_rglru_kernel, seq=seq),
        grid=(batch, nt),
        in_specs=[pl.BlockSpec((seq, A_TILE), lambda b, c: (b, OFF_A // A_TILE + c)),
                  pl.BlockSpec((CONV_WIDTH, w), lambda b, c: (0, c)),
                  pl.BlockSpec((1, w), lambda b, c: (0, c)),
                  pl.BlockSpec((1, w, 4 * w), lambda b, c: (c, 0, 0)),
                  pl.BlockSpec((1, 1, 4 * w), lambda b, c: (c, 0, 0)),
                  pl.BlockSpec((2, w), lambda b, c: (0, c))],
        out_specs=pl.BlockSpec((seq, w), lambda b, c: (b, c)),
        out_shape=jax.ShapeDtypeStruct((batch * seq, D_MODEL), BF16),
        scratch_shapes=[pltpu.VMEM((seq + 16, w), F32),
                        scan_scratch, scan_scratch, scan_scratch, scan_scratch],
        compiler_params=_cparams(("parallel", "parallel")),
        name="rglru",
    )(proj, cw, cb, wd, bd, lam)


def _rope_mxu(x, perm, c, s):
    return x * c + _dot(x.astype(BF16), perm) * s


def _rope_perm(width, rot, period):
    half = rot // 2
    lane = jnp.arange(width)
    j = lane % period
    src = jnp.where(j < half, lane + half, lane - half)
    sign = jnp.where(j < half, -1.0, 1.0) * (j < rot)
    return ((lane[:, None] == src[None, :]) * sign[None, :]).astype(BF16)


def _rope_cs_tables(seq, rot, period, theta):
    pos = jnp.arange(seq, dtype=F32)
    inv_freq = theta ** (-jnp.arange(0, rot, 2, dtype=F32) / rot)
    ang = pos[:, None] * inv_freq[None, :]
    cos, sin = jnp.cos(ang), jnp.sin(ang)
    pad = period - rot
    c = jnp.concatenate([cos, cos, jnp.ones((seq, pad), F32)], axis=1)
    s = jnp.concatenate([sin, sin, jnp.zeros((seq, pad), F32)], axis=1)
    rep = LANES // period
    return jnp.tile(c, (1, rep)), jnp.tile(s, (1, rep))


def _diff_prep_kernel(q_ref, k_ref, gq_ref, gk_ref, c_ref, s_ref, gm_ref, rp_ref,
                      qo_ref, ko_ref, *, qscale):
    c = jnp.concatenate([c_ref[...]] * (MXU_DIM // LANES), axis=1)
    sn = jnp.concatenate([s_ref[...]] * (MXU_DIM // LANES), axis=1)
    gm = gm_ref[...]
    rp = rp_ref[...]
    for src, g_ref, is_key in ((q_ref, gq_ref, False), (k_ref, gk_ref, True)):
        for blk in range(D_MODEL // MXU_DIM):
            x = src[:, blk * MXU_DIM:(blk + 1) * MXU_DIM].astype(F32)
            ms = _dot((x * x).astype(BF16), gm)
            xr = _rope_mxu(x * lax.rsqrt(ms + EPS) * g_ref[...], rp, c, sn)
            for h in range(MXU_DIM // LANES):
                xb = xr[:, h * LANES:(h + 1) * LANES]
                lo = blk * MXU_DIM + h * LANES
                if is_key:
                    ko_ref[lo:lo + LANES, :] = xb.T.astype(ko_ref.dtype)
                else:
                    qo_ref[:, lo:lo + LANES] = (xb * qscale).astype(qo_ref.dtype)


def _diff_prep(proj, gq, gk, tabs, gm, rp, seq, tm, qscale):
    m = proj.shape[0]
    nrow = seq // tm
    tab_spec = pl.BlockSpec((tm, LANES), lambda i: (i % nrow, 0))
    return pl.pallas_call(
        functools.partial(_diff_prep_kernel, qscale=qscale),
        grid=(m // tm,),
        in_specs=[pl.BlockSpec((tm, D_MODEL), lambda i: (i, OFF_BQ // D_MODEL)),
                  pl.BlockSpec((tm, D_MODEL), lambda i: (i, OFF_BK // D_MODEL)),
                  pl.BlockSpec((1, MXU_DIM), lambda i: (0, 0)),
                  pl.BlockSpec((1, MXU_DIM), lambda i: (0, 0)),
                  tab_spec, tab_spec,
                  pl.BlockSpec((MXU_DIM, MXU_DIM), lambda i: (0, 0)),
                  pl.BlockSpec((MXU_DIM, MXU_DIM), lambda i: (0, 0))],
        out_specs=[pl.BlockSpec((tm, D_MODEL), lambda i: (i, 0)),
                   pl.BlockSpec((None, D_MODEL, tm), lambda i: (i // nrow, 0, i % nrow))],
        out_shape=[jax.ShapeDtypeStruct((m, D_MODEL), BF16),
                   jax.ShapeDtypeStruct((m // seq, D_MODEL, seq), BF16)],
        compiler_params=_cparams(("parallel",)),
        name="diff_prep",
    )(proj, proj, gq, gk, *tabs, gm, rp)


def _mla_prep_kernel(c_ref, qag_ref, wuq_ref, kvag_ref, wukv_ref, gq_ref, gk_ref,
                     c_tab, s_tab, rp_ref, qo_ref, ko_ref, vo_ref, *, qscale):
    dh = MLA_NOPE + MLA_ROPE
    ct, st, rp = c_tab[...], s_tab[...], rp_ref[...]

    def sumsq(x):
        return _dot((x * x).astype(BF16), jnp.ones((x.shape[1], LANES), BF16))

    def rms_wide(x, g):
        inv = lax.rsqrt(sumsq(x) * (1.0 / x.shape[1]) + EPS)
        return x * jnp.concatenate([inv] * (x.shape[1] // LANES), axis=1) * g

    cq = c_ref[:, 0:MLA_Q_LORA].astype(F32)
    ckv = c_ref[:, MLA_Q_LORA:MLA_Q_LORA + MLA_KV_LORA].astype(F32)
    kpe = c_ref[:, MLA_Q_LORA + MLA_KV_LORA:C_WIDTH].astype(F32)
    q = _dot(rms_wide(cq, qag_ref[...]).astype(BF16), wuq_ref[...])
    kv = _dot(rms_wide(ckv, kvag_ref[...]).astype(BF16), wukv_ref[...])
    gq_n, gq_r = gq_ref[:, 0:LANES], gq_ref[:, LANES:2 * LANES]
    gk_n, gk_r = gk_ref[:, 0:LANES], gk_ref[:, LANES:2 * LANES]
    kpe_ss = sumsq(kpe)
    for h in range(MLA_HEADS):
        lo = h * MLA_HEAD_PAD
        qn, qr = q[:, lo:lo + LANES], q[:, lo + LANES:lo + 2 * LANES]
        qh = q[:, lo:lo + 2 * LANES]
        inv = lax.rsqrt(jnp.sum(qh * qh, axis=-1, keepdims=True) * (1.0 / dh) + EPS) * qscale
        qo_ref[:, lo:lo + LANES] = (qn * inv * gq_n).astype(qo_ref.dtype)
        qo_ref[:, lo + LANES:lo + 2 * LANES] = _rope_mxu(
            qr * inv * gq_r, rp, ct, st).astype(qo_ref.dtype)
        kn = kv[:, h * LANES:(h + 1) * LANES]
        inv = lax.rsqrt((jnp.sum(kn * kn, axis=-1, keepdims=True) + kpe_ss) * (1.0 / dh) + EPS)
        ko_ref[lo:lo + LANES, :] = (kn * inv * gk_n).T.astype(ko_ref.dtype)
        ko_ref[lo + LANES:lo + 2 * LANES, :] = _rope_mxu(
            kpe * inv * gk_r, rp, ct, st).T.astype(ko_ref.dtype)
    vo_ref[...] = kv[:, MLA_HEADS * LANES:].astype(vo_ref.dtype)


def _mla_prep(proj, qag, wuq, kvag, wukv, gq, gk, tabs, rp, seq, tm, qscale):
    m = proj.shape[0]
    nrow = seq // tm
    hp = MLA_HEADS * MLA_HEAD_PAD
    tab_spec = pl.BlockSpec((tm, LANES), lambda i: (i % nrow, 0))
    const = lambda shape: pl.BlockSpec(shape, lambda i: (0, 0))
    return pl.pallas_call(
        functools.partial(_mla_prep_kernel, qscale=qscale),
        grid=(m // tm,),
        in_specs=[pl.BlockSpec((tm, C_WIDTH), lambda i: (i, OFF_C // C_WIDTH)),
                  const((1, MLA_Q_LORA)), const((MLA_Q_LORA, hp)),
                  const((1, MLA_KV_LORA)), const((MLA_KV_LORA, 2 * MLA_HEADS * LANES)),
                  const((1, MLA_HEAD_PAD)), const((1, MLA_HEAD_PAD)),
                  tab_spec, tab_spec, const((LANES, LANES))],
        out_specs=[pl.BlockSpec((tm, hp), lambda i: (i, 0)),
                   pl.BlockSpec((None, hp, tm), lambda i: (i // nrow, 0, i % nrow)),
                   pl.BlockSpec((tm, MLA_HEADS * MLA_V), lambda i: (i, 0))],
        out_shape=[jax.ShapeDtypeStruct((m, hp), BF16),
                   jax.ShapeDtypeStruct((m // seq, hp, seq), BF16),
                   jax.ShapeDtypeStruct((m, MLA_HEADS * MLA_V), BF16)],
        compiler_params=_cparams(("parallel",)),
        name="mla_prep",
    )(proj, qag, wuq, kvag, wukv, gq, gk, *tabs, rp)


def _attn_kernel(q_ref, k_ref, v_ref, lam_ref, sg_ref, o_ref, vext, m_scr, acc_scr,
                 *, seq, tk, diff, out_scale):
    tq = q_ref.shape[0]

    @pl.when(pl.program_id(2) == 0)
    def _():
        vext[:, 0:LANES] = v_ref[...]
        vext[:, LANES:2 * LANES] = jnp.ones((seq, LANES), BF16)

    q = q_ref[...]
    if diff:
        lane = lax.broadcasted_iota(jnp.int32, q.shape, 1)
        zero = jnp.zeros_like(q)
        q = jnp.concatenate([jnp.where(lane < DIFF_DH, q, zero),
                             jnp.where(lane >= DIFF_DH, q, zero)], axis=0)
    rows = q.shape[0]
    m_scr[...] = jnp.full((rows, LANES), -1e30, F32)
    acc_scr[...] = jnp.zeros((rows, 2 * LANES), F32)

    def body(j, carry):
        off = pl.multiple_of(j * tk, tk)
        s = _dot(q, k_ref[:, pl.ds(off, tk)])
        m_prev = m_scr[...]
        m_new = jnp.maximum(m_prev, jnp.max(s, axis=1, keepdims=True))
        alpha = jnp.exp2(m_prev - m_new)
        p = jnp.exp2(s - m_new[:, 0:1]).astype(BF16)
        acc_scr[...] = (acc_scr[...] * jnp.concatenate([alpha, alpha], axis=1)
                        + _dot(p, vext[pl.ds(off, tk), :]))
        m_scr[...] = m_new
        return carry

    lax.fori_loop(0, seq // tk, body, 0)

    acc = acc_scr[...]
    o = acc[:, 0:LANES] / acc[:, LANES:2 * LANES]
    if diff:
        o = o[0:tq, :] - lam_ref[...] * o[tq:2 * tq, :]
        o = _rms(o, sg_ref[...]) * out_scale
    o_ref[...] = o.astype(o_ref.dtype)


def _attn_bounded_kernel(q_ref, k_ref, v_ref, lam_ref, sg_ref, o_ref, vext,
                         *, seq, tk, diff, out_scale):
    tq = q_ref.shape[0]

    @pl.when(pl.program_id(2) == 0)
    def _():
        vext[:, 0:LANES] = v_ref[...]
        vext[:, LANES:2 * LANES] = jnp.ones((seq, LANES), BF16)

    q = q_ref[...]
    if diff:
        lane = lax.broadcasted_iota(jnp.int32, q.shape, 1)
        zero = jnp.zeros_like(q)
        q = jnp.concatenate([jnp.where(lane < DIFF_DH, q, zero),
                             jnp.where(lane >= DIFF_DH, q, zero)], axis=0)
    rb = ATTN_ROW_BLOCK

    def softmax_rows(lo):
        acc = None
        for j in range(seq // tk):
            p = jnp.exp2(_dot(q[lo:lo + rb, :], k_ref[:, j * tk:(j + 1) * tk])).astype(BF16)
            pv = _dot(p, vext[j * tk:(j + 1) * tk, :])
            acc = pv if acc is None else acc + pv
        return acc[:, 0:LANES] / acc[:, LANES:2 * LANES]

    for r in range(tq // rb):
        o = softmax_rows(r * rb)
        if diff:
            o = o - lam_ref[...] * softmax_rows(tq + r * rb)
            o = _rms(o, sg_ref[...]) * out_scale
        o_ref[r * rb:(r + 1) * rb, :] = o.astype(o_ref.dtype)


def _score_bound(gq, gk, dh):
    return 1.02 * LOG2E * math.sqrt(dh) * jnp.max(jnp.abs(gq)) * jnp.max(jnp.abs(gk))


def _attention(q, k, v, lam, sg, *, v_off, batch, seq, heads, dq, tq, tk, diff, out_scale, bounded):
    nq = seq // tq
    rows = 2 * tq if diff else tq
    vblk = v_off // LANES
    body = _attn_bounded_kernel if bounded else _attn_kernel
    return pl.pallas_call(
        functools.partial(body, seq=seq, tk=tk, diff=diff, out_scale=out_scale),
        grid=(batch, heads, nq),
        in_specs=[pl.BlockSpec((tq, dq), lambda b, h, i: (b * nq + i, h)),
                  pl.BlockSpec((None, dq, seq), lambda b, h, i: (b, h, 0)),
                  pl.BlockSpec((seq, LANES), lambda b, h, i: (b, vblk + h)),
                  pl.BlockSpec((1, LANES), lambda b, h, i: (0, 0)),
                  pl.BlockSpec((1, LANES), lambda b, h, i: (0, 0))],
        out_specs=pl.BlockSpec((tq, LANES), lambda b, h, i: (b * nq + i, h)),
        out_shape=jax.ShapeDtypeStruct((batch * seq, heads * LANES), BF16),
        scratch_shapes=[pltpu.VMEM((seq, 2 * LANES), BF16)] + ([] if bounded else [
            pltpu.VMEM((rows, LANES), F32), pltpu.VMEM((rows, 2 * LANES), F32)]),
        compiler_params=_cparams(("parallel", "parallel", "arbitrary")),
        name=("diff_attn" if diff else "mla_attn") + ("_bounded" if bounded else "_online"),
    )(q, k, v, lam, sg)


def _retention_kernel(d_ref, lg_ref, gn_ref, c_tab, s_tab, rp_ref,
                      o_ref, qs, kts, o_fwd, o_bwd, state, *, seq):
    ch = RET_CHUNK if seq % RET_CHUNK == 0 else seq
    nc = seq // ch
    ct, st, rp = c_tab[...], s_tab[...], rp_ref[...]
    voff = 2 * LANES
    goff = voff + 2 * RET_DV
    qs[...] = _rope_mxu(d_ref[:, 0:LANES].astype(F32), rp, ct, st)
    kt = (_rope_mxu(d_ref[:, LANES:2 * LANES].astype(F32), rp, ct, st) * (RET_DK ** -0.5)).T
    row = lax.broadcasted_iota(jnp.int32, kt.shape, 0)
    kts[0] = jnp.where(row < RET_DK, kt, 0.0)
    kts[1] = jnp.where(row >= RET_DK, kt, 0.0)
    state[...] = jnp.zeros(state.shape, F32)

    ri = lax.broadcasted_iota(jnp.int32, (ch, ch), 0).astype(F32)
    ci = lax.broadcasted_iota(jnp.int32, (ch, ch), 1).astype(F32)
    qidx = lax.broadcasted_iota(jnp.int32, (ch, 1), 0).astype(F32)
    kidx = lax.broadcasted_iota(jnp.int32, (1, ch), 1).astype(F32)
    dec = []
    for hh in range(2):
        lg = lg_ref[hh:hh + 1, 0:1]
        dec.append(dict(
            intra=jnp.exp(lg * jnp.abs(ri - ci)),
            tail=jnp.exp(lg * (ch - 1.0 - kidx)),
            head=jnp.exp(lg * kidx),
            qf=jnp.exp(lg * (qidx + 1.0)),
            qb=jnp.exp(lg * (ch - qidx)),
            chunk=jnp.exp(lg * float(ch))))

    def sweep(t, carry):
        of = pl.multiple_of(t * ch, ch)
        ob = pl.multiple_of((nc - 1 - t) * ch, ch)
        qf = qs[pl.ds(of, ch), :]
        qb = qs[pl.ds(ob, ch), :]
        for hh in range(2):
            d = dec[hh]
            vlo = hh * RET_DV
            ktf = kts[hh, :, pl.ds(of, ch)]
            vf = d_ref[pl.ds(of, ch), voff + vlo:voff + vlo + RET_DV]
            sc = _dot(qf.astype(BF16), ktf.astype(BF16)) * d["intra"]
            past = state[2 * hh]
            o_fwd[hh, pl.ds(of, ch), :] = (_dot(sc.astype(BF16), vf)
                                           + _dot((qf * d["qf"]).astype(BF16), past.astype(BF16)))
            state[2 * hh] = past * d["chunk"] + _dot((ktf * d["tail"]).astype(BF16), vf)
            ktb = kts[hh, :, pl.ds(ob, ch)]
            vb = d_ref[pl.ds(ob, ch), voff + vlo:voff + vlo + RET_DV]
            fut = state[2 * hh + 1]
            o_bwd[hh, pl.ds(ob, ch), :] = _dot((qb * d["qb"]).astype(BF16), fut.astype(BF16))
            state[2 * hh + 1] = fut * d["chunk"] + _dot((ktb * d["head"]).astype(BF16), vb)
        return carry

    lax.fori_loop(0, nc, sweep, 0, unroll=2 if nc % 2 == 0 else 1)

    lane_mean = jnp.full((RET_DV, RET_DV), 1.0 / RET_DV, BF16)

    def fin(n, carry):
        off = pl.multiple_of(n * ch, ch)
        for hh in range(2):
            vlo = hh * RET_DV
            o = o_fwd[hh, pl.ds(off, ch), :] + o_bwd[hh, pl.ds(off, ch), :]
            oc = o - _dot(o.astype(BF16), lane_mean)
            var = _dot((oc * oc).astype(BF16), lane_mean)
            y = oc * lax.rsqrt(var + EPS) * gn_ref[...]
            g = d_ref[pl.ds(off, ch), goff + vlo:goff + vlo + RET_DV].astype(F32)
            o_ref[pl.ds(off, ch), vlo:vlo + RET_DV] = (g * _sigmoid(g) * y).astype(o_ref.dtype)
        return carry

    lax.fori_loop(0, nc, fin, 0, unroll=2 if nc % 2 == 0 else 1)


def _retention(proj, lg, gn, tabs, rp, batch, seq):
    npair = RET_HEADS // 2
    pw = 2 * RET_DV
    tab_spec = pl.BlockSpec((seq, LANES), lambda b, j: (0, 0))
    return pl.pallas_call(
        functools.partial(_retention_kernel, seq=seq),
        grid=(batch, npair),
        in_specs=[pl.BlockSpec((seq, D_PAIR), lambda b, j: (b, OFF_D // D_PAIR + j)),
                  pl.BlockSpec((None, 2, LANES), lambda b, j: (j, 0, 0)),
                  pl.BlockSpec((1, RET_DV), lambda b, j: (0, 0)),
                  tab_spec, tab_spec, pl.BlockSpec((LANES, LANES), lambda b, j: (0, 0))],
        out_specs=pl.BlockSpec((seq, pw), lambda b, j: (b, j)),
        out_shape=jax.ShapeDtypeStruct((batch * seq, RET_HEADS * RET_DV), BF16),
        scratch_shapes=[pltpu.VMEM((seq, LANES), F32),
                        pltpu.VMEM((2, LANES, seq), F32),
                        pltpu.VMEM((2, seq, RET_DV), F32),
                        pltpu.VMEM((2, seq, RET_DV), F32),
                        pltpu.VMEM((4, LANES, RET_DV), F32)],
        compiler_params=_cparams(("parallel", "parallel")),
        name="retention",
    )(proj, lg, gn, *tabs, rp)


def _merge_kernel(x_ref, gl_ref, gb_ref, ya_ref, yb_ref, yc_ref, yd_ref,
                  wa_ref, wb_ref, wc_ref, wd_ref, wo_ref, o_ref):
    for r in range(x_ref.shape[0] // NORM_ROW_BLOCK):
        rows = slice(r * NORM_ROW_BLOCK, (r + 1) * NORM_ROW_BLOCK)
        merged = None
        for n, (y_ref, w_ref) in enumerate(((ya_ref, wa_ref), (yb_ref, wb_ref),
                                            (yc_ref, wc_ref), (yd_ref, wd_ref))):
            lo = n * D_MODEL
            gate = _sigmoid(gl_ref[rows, lo:lo + D_MODEL].astype(F32)
                            + gb_ref[:, lo:lo + D_MODEL])
            term = gate * _dot(y_ref[rows, :], w_ref[...])
            merged = term if merged is None else merged + term
        o_ref[rows, :] = x_ref[rows, :] + _dot(merged.astype(BF16), wo_ref[...])


def _merge(x2, proj, gb, ys, ws, wo, tm):
    m = x2.shape[0]
    row = lambda width: pl.BlockSpec((tm, width), lambda i: (i, 0))
    wspec = pl.BlockSpec((D_MODEL, D_MODEL), lambda i: (0, 0), pipeline_mode=pl.Buffered(1))
    return pl.pallas_call(
        _merge_kernel,
        grid=(m // tm,),
        in_specs=[row(D_MODEL), row(N_BRANCH * D_MODEL),
                  pl.BlockSpec((1, N_BRANCH * D_MODEL), lambda i: (0, 0)),
                  row(D_MODEL), row(D_MODEL), row(D_MODEL), row(D_MODEL),
                  wspec, wspec, wspec, wspec, wspec],
        out_specs=row(D_MODEL),
        out_shape=jax.ShapeDtypeStruct((m, D_MODEL), F32),
        compiler_params=_cparams(("parallel",)),
        name="merge",
    )(x2, proj, gb, *ys, *ws, wo)


def _ffn_kernel(x_ref, g2_ref, w1_ref, w2_ref, g3_ref, wpg_ref, p_ref, wpe_ref, o_ref,
                acc_ref):
    j = pl.program_id(1)
    blocks = [slice(r * NORM_ROW_BLOCK, (r + 1) * NORM_ROW_BLOCK)
              for r in range(x_ref.shape[0] // NORM_ROW_BLOCK)]

    @pl.when(j == 0)
    def _():
        acc_ref[...] = jnp.zeros_like(acc_ref)

    for rows in blocks:
        xn = _rms(x_ref[rows, :], g2_ref[...]).astype(BF16)
        h = jnp.maximum(_dot(xn, w1_ref[...]), 0.0)
        acc_ref[rows, :] += _dot((h * h).astype(BF16), w2_ref[...])

    @pl.when(j == pl.num_programs(1) - 1)
    def _():
        for rows in blocks:
            x = x_ref[rows, :] + acc_ref[rows, :]
            gate = _sigmoid(_dot(_rms(x, g3_ref[...]).astype(BF16), wpg_ref[...]))
            o_ref[rows, :] = x + gate * _dot(p_ref[rows, :].astype(BF16), wpe_ref[...])


def _ffn(x2, g2, w1, w2, g3, wpg, p2, layer, wpe, tm, tf):
    m = x2.shape[0]
    dff = w1.shape[1]
    pdim = p2.shape[1]
    p_row0 = layer * (m // tm)
    return pl.pallas_call(
        _ffn_kernel,
        grid=(m // tm, dff // tf),
        in_specs=[pl.BlockSpec((tm, D_MODEL), lambda i, j: (i, 0)),
                  pl.BlockSpec((1, D_MODEL), lambda i, j: (0, 0)),
                  pl.BlockSpec((D_MODEL, tf), lambda i, j: (0, j)),
                  pl.BlockSpec((tf, D_MODEL), lambda i, j: (j, 0)),
                  pl.BlockSpec((1, D_MODEL), lambda i, j: (0, 0)),
                  pl.BlockSpec((D_MODEL, D_MODEL), lambda i, j: (0, 0)),
                  pl.BlockSpec((tm, pdim), lambda i, j: (p_row0 + i, 0)),
                  pl.BlockSpec((pdim, D_MODEL), lambda i, j: (0, 0))],
        out_specs=pl.BlockSpec((tm, D_MODEL), lambda i, j: (i, 0)),
        out_shape=jax.ShapeDtypeStruct((m, D_MODEL), F32),
        scratch_shapes=[pltpu.VMEM((tm, D_MODEL), F32)],
        compiler_params=_cparams(("parallel", "arbitrary")),
        name="ffn_ple",
    )(x2, g2, w1, w2, g3, wpg, p2, wpe)


def _block_diag(blocks):
    n, c, d = blocks.shape
    eye = jnp.eye(n, dtype=blocks.dtype)
    return jnp.einsum('ncd,nm->ncmd', blocks, eye).reshape(n * c, n * d)


def _permute_w_in(w_in):
    sizes = (1024, 1024, 1024, 1024, 1024, 384, 256, 64, 512, 512, 1024, 1024, 4096)
    offs = [0]
    for s in sizes:
        offs.append(offs[-1] + s)
    (a_x, a_g, b_q, b_k, b_v, c_q, c_kv, c_kpe, d_q, d_k, d_v, d_g, gates) = [
        w_in[:, offs[n]:offs[n + 1]] for n in range(len(sizes))]
    pad = jnp.zeros((w_in.shape[0], C_WIDTH - 704), w_in.dtype)
    a_tiles = []
    for c in range(D_MODEL // MXU_DIM):
        cols = slice(c * MXU_DIM, (c + 1) * MXU_DIM)
        a_tiles += [a_x[:, cols], a_g[:, cols]]
    d_pairs = []
    for j in range(RET_HEADS // 2):
        qk = slice(2 * j * RET_DK, 2 * (j + 1) * RET_DK)
        vg = slice(2 * j * RET_DV, 2 * (j + 1) * RET_DV)
        d_pairs += [d_q[:, qk], d_k[:, qk], d_v[:, vg], d_g[:, vg]]
    return jnp.concatenate([gates] + a_tiles + [b_q, b_k, b_v] + d_pairs
                           + [c_q, c_kv, c_kpe, pad], axis=1).astype(BF16)


def _lru_weights(wa, ba, wx, bx):
    w = MXU_DIM
    nt = D_MODEL // w
    dense = [_block_diag(m) for m in (wa[0], wx[0], wa[1], wx[1])]
    wd = jnp.stack([jnp.concatenate([d[c * w:(c + 1) * w, c * w:(c + 1) * w] for d in dense], axis=1)
                    for c in range(nt)]).astype(BF16)
    bias = (ba[0], bx[0], ba[1], bx[1])
    bd = jnp.stack([jnp.concatenate([b[c * w:(c + 1) * w] for b in bias])[None, :]
                    for c in range(nt)])
    return wd, bd


def _mla_weights(wuq, wukv, gq, gk):
    dh = MLA_NOPE + MLA_ROPE
    zpad = MLA_HEAD_PAD - dh
    wq = wuq.reshape(MLA_Q_LORA, MLA_HEADS, dh)
    wq = jnp.pad(wq, ((0, 0), (0, 0), (0, zpad))).reshape(MLA_Q_LORA, MLA_HEADS * MLA_HEAD_PAD)
    wkv = wukv.reshape(MLA_KV_LORA, MLA_HEADS, MLA_NOPE + MLA_V)
    wk = wkv[:, :, :MLA_NOPE].reshape(MLA_KV_LORA, MLA_HEADS * MLA_NOPE)
    wv = wkv[:, :, MLA_NOPE:].reshape(MLA_KV_LORA, MLA_HEADS * MLA_V)
    gqp = jnp.pad(gq, (0, zpad))[None, :]
    gkp = jnp.pad(gk, (0, zpad))[None, :]
    return wq.astype(BF16), jnp.concatenate([wk, wv], axis=1).astype(BF16), gqp, gkp


def kernel(x, p, norm1_g, w_in, gate_b, conv_w, conv_b, lru_wa, lru_ba, lru_wx, lru_bx, lru_lambda,
           diff_q_g, diff_k_g, diff_lam, diff_sub_g, mla_qa_g, mla_wuq, mla_kva_g, mla_wukv,
           mla_q_g, mla_k_g, ret_gn_g, w_br_a, w_br_b, w_br_c, w_br_d, w_out, norm2_g,
           w_ff1, w_ff2, norm3_g, w_ple_gate, w_ple_proj):
    batch, seq, d_model = x.shape
    depth = w_in.shape[0]
    assert d_model == D_MODEL and seq % (8 * N_SEG) == 0
    m = batch * seq
    tm_big = min(1024, seq)
    tm_small = min(512, seq)
    tq = min(2048, seq)
    tk = min(512, seq)

    tabs_b = _rope_cs_tables(seq, DIFF_ROT, DIFF_DH, ROPE_THETA)
    tabs_c = _rope_cs_tables(seq, MLA_ROPE, LANES, ROPE_THETA)
    rp_b = _rope_perm(MXU_DIM, DIFF_ROT, DIFF_DH)
    rp_c = _rope_perm(LANES, MLA_ROPE, LANES)
    tabs_d = _rope_cs_tables(seq, RET_DK, RET_DK, RET_THETA)
    rp_d = _rope_perm(LANES, RET_DK, RET_DK)
    log_g = jnp.log1p(-jnp.exp2(-5.0 - jnp.arange(RET_HEADS, dtype=F32)))
    lg = jnp.broadcast_to(log_g.reshape(RET_HEADS // 2, 2, 1), (RET_HEADS // 2, 2, LANES))
    grp = jnp.arange(MXU_DIM) // DIFF_DH
    gm = ((grp[:, None] == grp[None, :]).astype(F32) / DIFF_DH).astype(BF16)
    ones_row = jnp.ones((1, LANES), F32)

    x2 = x.reshape(m, d_model)
    for i in range(depth):
        proj = _inproj(x2, norm1_g[i][None, :], _permute_w_in(w_in[i]), tm_big, N_PROJ // 3)

        wd, bd = _lru_weights(lru_wa[i], lru_ba[i], lru_wx[i], lru_bx[i])
        y_a = _rglru(proj, conv_w[i], conv_b[i][None, :], wd, bd, lru_lambda[i], batch, seq)

        lam_init = 0.8 - 0.6 * math.exp(-0.3 * i)
        lp = diff_lam[i].astype(F32)
        lam = jnp.exp(jnp.sum(lp[0] * lp[1])) - jnp.exp(jnp.sum(lp[2] * lp[3])) + lam_init
        gq = jnp.tile(diff_q_g[i], MXU_DIM // DIFF_DH)[None, :]
        gk = jnp.tile(diff_k_g[i], MXU_DIM // DIFF_DH)[None, :]
        qb, kb = _diff_prep(proj, gq, gk, tabs_b, gm, rp_b, seq, tm_small, DIFF_DH ** -0.5 * LOG2E)
        attn_b = functools.partial(_attention, v_off=OFF_BV, batch=batch, seq=seq, heads=DIFF_HEADS,
                                   dq=LANES, tq=tq, tk=tk, diff=True, out_scale=1.0 - lam_init)
        y_b = lax.cond(
            _score_bound(diff_q_g[i], diff_k_g[i], DIFF_DH) <= MAX_SCORE_LOG2,
            functools.partial(attn_b, bounded=True), functools.partial(attn_b, bounded=False),
            qb, kb, proj, jnp.full((1, LANES), lam, F32), diff_sub_g[i][None, :])

        wq, wkv, gqp, gkp = _mla_weights(mla_wuq[i], mla_wukv[i], mla_q_g[i], mla_k_g[i])
        qc, kc, vc = _mla_prep(proj, mla_qa_g[i][None, :], wq, mla_kva_g[i][None, :], wkv,
                               gqp, gkp, tabs_c, rp_c, seq, tm_small,
                               (MLA_NOPE + MLA_ROPE) ** -0.5 * LOG2E)
        attn_c = functools.partial(_attention, v_off=0, batch=batch, seq=seq, heads=MLA_HEADS,
                                   dq=MLA_HEAD_PAD, tq=min(2 * tq, seq), tk=tk, diff=False,
                                   out_scale=1.0)
        y_c = lax.cond(
            _score_bound(mla_q_g[i], mla_k_g[i], MLA_NOPE + MLA_ROPE) <= MAX_SCORE_LOG2,
            functools.partial(attn_c, bounded=True), functools.partial(attn_c, bounded=False),
            qc, kc, vc, ones_row, ones_row)

        y_d = _retention(proj, lg, ret_gn_g[i][None, :], tabs_d, rp_d, batch, seq)

        x2 = _merge(x2, proj, gate_b[i].reshape(1, N_BRANCH * D_MODEL), (y_a, y_b, y_c, y_d),
                    tuple(w[i].astype(BF16) for w in (w_br_a, w_br_b, w_br_c, w_br_d)),
                    w_out[i].astype(BF16), tm_small)

        x2 = _ffn(x2, norm2_g[i][None, :], w_ff1[i].astype(BF16), w_ff2[i].astype(BF16),
                  norm3_g[i][None, :], w_ple_gate[i].astype(BF16),
                  p.reshape(depth * m, p.shape[-1]), i, w_ple_proj[i].astype(BF16), tm_big, 2048)
    return x2.reshape(batch, seq, d_model)
```

```python
import functools
import math
from typing import NamedTuple

import jax
import jax.numpy as jnp
from jax import lax
from jax.experimental import pallas as pl
from jax.experimental.pallas import tpu as pltpu

F32 = jnp.float32
BF16 = jnp.bfloat16

EPS = 1e-6
ROPE_THETA = 500000.0
RET_THETA = 10000.0
LRU_C = 8.0
LOG2E = 1.4426950408889634

LANES = 128
MXU_DIM = 256
VMEM_LIMIT = 56 * 1024 * 1024

D_MODEL = 1024
N_BRANCH = 4
LRU_BLOCKS = 16
CONV_WIDTH = 4
DIFF_HEADS = 8
DIFF_DH = 64
DIFF_ROT = 16
MLA_HEADS = 8
MLA_NOPE = 128
MLA_ROPE = 64
MLA_V = 128
MLA_Q_LORA = 384
MLA_KV_LORA = 256
MLA_HEAD_PAD = 256
RET_HEADS = 8
RET_DK = 64
RET_DV = 128
RET_CHUNK = 256
ATTN_ROW_BLOCK = 128
NORM_ROW_BLOCK = 256
MAX_SCORE_LOG2 = 60.0
N_SEG = 8
SEG_PAD = 4

OFF_GATE = 0
OFF_A = 4096
A_TILE = 2 * MXU_DIM
OFF_BQ = 6144
OFF_BK = 7168
OFF_BV = 8192
OFF_D = 9216
D_PAIR = 4 * RET_DK + 4 * RET_DV
OFF_C = 12288
C_WIDTH = 768
N_PROJ = OFF_C + C_WIDTH


def _cparams(sem):
    return pltpu.CompilerParams(dimension_semantics=sem, vmem_limit_bytes=VMEM_LIMIT)


def _sigmoid(x):
    return 1.0 / (1.0 + jnp.exp2(x * -LOG2E))


def _rms(x, g):
    return x * lax.rsqrt(jnp.mean(x * x, axis=-1, keepdims=True) + EPS) * g


def _dot(a, b):
    return jnp.dot(a, b, preferred_element_type=F32)


def _dot_nt(a, b):
    return lax.dot_general(a, b, (((1,), (1,)), ((), ())), preferred_element_type=F32)


def _dot_tn(a, b):
    return lax.dot_general(a, b, (((0,), (0,)), ((), ())), preferred_element_type=F32)


def _inproj_kernel(x_ref, g_ref, w_ref, o_ref):
    for r in range(x_ref.shape[0] // NORM_ROW_BLOCK):
        rows = slice(r * NORM_ROW_BLOCK, (r + 1) * NORM_ROW_BLOCK)
        xn = _rms(x_ref[rows, :], g_ref[...]).astype(BF16)
        o_ref[rows, :] = _dot(xn, w_ref[...]).astype(o_ref.dtype)


def _inproj(x2, g, w, tm, tn):
    m, k = x2.shape
    n = w.shape[1]
    return pl.pallas_call(
        _inproj_kernel,
        grid=(m // tm, n // tn),
        in_specs=[pl.BlockSpec((tm, k), lambda i, j: (i, 0)),
                  pl.BlockSpec((1, k), lambda i, j: (0, 0)),
                  pl.BlockSpec((k, tn), lambda i, j: (0, j))],
        out_specs=pl.BlockSpec((tm, tn), lambda i, j: (i, j)),
        out_shape=jax.ShapeDtypeStruct((m, n), BF16),
        compiler_params=_cparams(("parallel", "arbitrary")),
        name="inproj",
    )(x2, g, w)


def _gelu_tanh(x):
    c = 0.7978845608028654
    return (0.5 * x) * (1.0 + jnp.tanh(x * (c + (c * 0.044715) * (x * x))))


def _rglru_kernel(xg_ref, cw_ref, cb_ref, wd_ref, bd_ref, lam_ref, o_ref,
                  xpad, a0, u0, a1, u1, *, seq):
    w = MXU_DIM
    nslab = w // LANES
    seg = seq // N_SEG
    pitch = seg + SEG_PAD
    left = CONV_WIDTH // 2

    xpad[0:8, :] = jnp.zeros((8, w), F32)
    xpad[seq + 8:seq + 16, :] = jnp.zeros((8, w), F32)
    for s in range(N_SEG):
        xpad[8 + s * seg:8 + (s + 1) * seg, :] = xg_ref[s * seg:(s + 1) * seg, 0:w].astype(F32)

    lam = lam_ref[...]
    nlam = -lam
    nls = (-LRU_C * LOG2E) * (jnp.maximum(nlam, 0.0) + jnp.log(1.0 + jnp.exp(-jnp.abs(nlam))))

    a_scr = (a0, a1)
    u_scr = (u0, u1)
    for s in range(N_SEG):
        base = 8 + s * seg - left
        xc = cb_ref[...]
        for t in range(CONV_WIDTH):
            xc = xc + xpad[base + t:base + t + seg, :] * cw_ref[t:t + 1, :]
        z = _dot(xc.astype(BF16), wd_ref[0]) + bd_ref[0]
        for d in range(2):
            r = _sigmoid(z[:, (2 * d) * w:(2 * d + 1) * w])
            gi = _sigmoid(z[:, (2 * d + 1) * w:(2 * d + 2) * w])
            a = jnp.exp2(nls[d:d + 1, :] * r)
            v = 1.0 - a * a
            u = (v * lax.rsqrt(jnp.maximum(v, 1e-30))) * (gi * xc)
            for sl in range(nslab):
                a_scr[d][sl, s * pitch:s * pitch + seg, :] = a[:, sl * LANES:(sl + 1) * LANES]
                u_scr[d][sl, s * pitch:s * pitch + seg, :] = u[:, sl * LANES:(sl + 1) * LANES]

    def rows(t):
        return pl.ds(t, N_SEG, stride=pitch)

    def scan_body(t, carry):
        new = []
        tb = seg - 1 - t
        for sl in range(nslab):
            hf, pf, hb, pb = carry[sl]
            af = a0[sl, rows(t), :]
            hf = af * hf + u0[sl, rows(t), :]
            pf = pf * af
            u0[sl, rows(t), :] = hf
            a0[sl, rows(t), :] = pf
            ab = a1[sl, rows(tb), :]
            hb = ab * hb + u1[sl, rows(tb), :]
            pb = pb * ab
            u1[sl, rows(tb), :] = hb
            a1[sl, rows(tb), :] = pb
            new.append((hf, pf, hb, pb))
        return tuple(new)

    zero = jnp.zeros((N_SEG, LANES), F32)
    one = jnp.ones((N_SEG, LANES), F32)
    fin = lax.fori_loop(0, seg, scan_body, tuple((zero, one, zero, one) for _ in range(nslab)),
                        unroll=4)

    cf, cb = [], []
    for sl in range(nslab):
        hf, pf, hb, pb = fin[sl]
        c = jnp.zeros((1, LANES), F32)
        cfr = [c]
        for s in range(1, N_SEG):
            c = pf[s - 1:s, :] * c + hf[s - 1:s, :]
            cfr.append(c)
        cf.append(jnp.concatenate(cfr, axis=0))
        c = jnp.zeros((1, LANES), F32)
        cbr = [c]
        for s in range(N_SEG - 2, -1, -1):
            c = pb[s + 1:s + 2, :] * c + hb[s + 1:s + 2, :]
            cbr.append(c)
        cb.append(jnp.concatenate(cbr[::-1], axis=0))

    def fix_body(t, carry):
        for sl in range(nslab):
            hf = u0[sl, rows(t), :] + a0[sl, rows(t), :] * cf[sl]
            hb = u1[sl, rows(t), :] + a1[sl, rows(t), :] * cb[sl]
            u0[sl, rows(t), :] = hf + hb
        return carry

    lax.fori_loop(0, seg, fix_body, 0, unroll=4)

    for s in range(N_SEG):
        g = _gelu_tanh(xg_ref[s * seg:(s + 1) * seg, w:2 * w].astype(F32))
        for sl in range(nslab):
            h = u0[sl, s * pitch:s * pitch + seg, :]
            o_ref[s * seg:(s + 1) * seg, sl * LANES:(sl + 1) * LANES] = (
                h * g[:, sl * LANES:(sl + 1) * LANES]).astype(o_ref.dtype)


def _rglru(proj, cw, cb, wd, bd, lam, batch, seq):
    w = MXU_DIM
    nt = D_MODEL // w
    nslab = w // LANES
    rows = N_SEG * (seq // N_SEG + SEG_PAD)
    scan_scratch = pltpu.VMEM((nslab, rows, LANES), F32)
    return pl.pallas_call(
        functools.partial(_rglru_kernel, seq=seq),
        grid=(batch, nt),
        in_specs=[pl.BlockSpec((seq, A_TILE), lambda b, c: (b, OFF_A // A_TILE + c)),
                  pl.BlockSpec((CONV_WIDTH, w), lambda b, c: (0, c)),
                  pl.BlockSpec((1, w), lambda b, c: (0, c)),
                  pl.BlockSpec((1, w, 4 * w), lambda b, c: (c, 0, 0)),
                  pl.BlockSpec((1, 1, 4 * w), lambda b, c: (c, 0, 0)),
                  pl.BlockSpec((2, w), lambda b, c: (0, c))],
        out_specs=pl.BlockSpec((seq, w), lambda b, c: (b, c)),
        out_shape=jax.ShapeDtypeStruct((batch * seq, D_MODEL), BF16),
        scratch_shapes=[pltpu.VMEM((seq + 16, w), F32),
                        scan_scratch, scan_scratch, scan_scratch, scan_scratch],
        compiler_params=_cparams(("parallel", "parallel")),
        name="rglru",
    )(proj, cw, cb, wd, bd, lam)


def _rope_mxu(x, perm, c, s):
    return x * c + _dot(x.astype(BF16), perm) * s


def _rope_perm(width, rot, period):
    half = rot // 2
    lane = jnp.arange(width)
    j = lane % period
    src = jnp.where(j < half, lane + half, lane - half)
    sign = jnp.where(j < half, -1.0, 1.0) * (j < rot)
    return ((lane[:, None] == src[None, :]) * sign[None, :]).astype(BF16)


def _rope_cs_tables(seq, rot, period, theta):
    pos = jnp.arange(seq, dtype=F32)
    inv_freq = theta ** (-jnp.arange(0, rot, 2, dtype=F32) / rot)
    ang = pos[:, None] * inv_freq[None, :]
    cos, sin = jnp.cos(ang), jnp.sin(ang)
    pad = period - rot
    c = jnp.concatenate([cos, cos, jnp.ones((seq, pad), F32)], axis=1)
    s = jnp.concatenate([sin, sin, jnp.zeros((seq, pad), F32)], axis=1)
    rep = LANES // period
    return jnp.tile(c, (1, rep)), jnp.tile(s, (1, rep))


def _diff_prep_kernel(q_ref, k_ref, gq_ref, gk_ref, c_ref, s_ref, gm_ref, rp_ref,
                      qo_ref, ko_ref):
    c = jnp.concatenate([c_ref[...]] * (MXU_DIM // LANES), axis=1)
    sn = jnp.concatenate([s_ref[...]] * (MXU_DIM // LANES), axis=1)
    gm = gm_ref[...]
    rp = rp_ref[...]
    for src, g_ref, is_key in ((q_ref, gq_ref, False), (k_ref, gk_ref, True)):
        for blk in range(D_MODEL // MXU_DIM):
            x = src[:, blk * MXU_DIM:(blk + 1) * MXU_DIM].astype(F32)
            ms = _dot((x * x).astype(BF16), gm)
            xr = _rope_mxu(x * lax.rsqrt(ms + EPS) * g_ref[...], rp, c, sn)
            for h in range(MXU_DIM // LANES):
                xb = xr[:, h * LANES:(h + 1) * LANES]
                lo = blk * MXU_DIM + h * LANES
                if is_key:
                    ko_ref[lo:lo + LANES, :] = xb.T.astype(ko_ref.dtype)
                else:
                    qo_ref[:, lo:lo + LANES] = xb.astype(qo_ref.dtype)


def _diff_prep(proj, gq, gk, tabs, gm, rp, seq, tm):
    m = proj.shape[0]
    nrow = seq // tm
    tab_spec = pl.BlockSpec((tm, LANES), lambda i: (i % nrow, 0))
    return pl.pallas_call(
        _diff_prep_kernel,
        grid=(m // tm,),
        in_specs=[pl.BlockSpec((tm, D_MODEL), lambda i: (i, OFF_BQ // D_MODEL)),
                  pl.BlockSpec((tm, D_MODEL), lambda i: (i, OFF_BK // D_MODEL)),
                  pl.BlockSpec((1, MXU_DIM), lambda i: (0, 0)),
                  pl.BlockSpec((1, MXU_DIM), lambda i: (0, 0)),
                  tab_spec, tab_spec,
                  pl.BlockSpec((MXU_DIM, MXU_DIM), lambda i: (0, 0)),
                  pl.BlockSpec((MXU_DIM, MXU_DIM), lambda i: (0, 0))],
        out_specs=[pl.BlockSpec((tm, D_MODEL), lambda i: (i, 0)),
                   pl.BlockSpec((None, D_MODEL, tm), lambda i: (i // nrow, 0, i % nrow))],
        out_shape=[jax.ShapeDtypeStruct((m, D_MODEL), BF16),
                   jax.ShapeDtypeStruct((m // seq, D_MODEL, seq), BF16)],
        compiler_params=_cparams(("parallel",)),
        name="diff_prep",
    )(proj, proj, gq, gk, *tabs, gm, rp)


def _mla_prep_kernel(c_ref, qag_ref, wuq_ref, kvag_ref, wukv_ref, gq_ref, gk_ref,
                     c_tab, s_tab, rp_ref, qo_ref, ko_ref, vo_ref, *, qscale):
    dh = MLA_NOPE + MLA_ROPE
    ct, st, rp = c_tab[...], s_tab[...], rp_ref[...]

    def sumsq(x):
        return _dot((x * x).astype(BF16), jnp.ones((x.shape[1], LANES), BF16))

    def rms_wide(x, g):
        inv = lax.rsqrt(sumsq(x) * (1.0 / x.shape[1]) + EPS)
        return x * jnp.concatenate([inv] * (x.shape[1] // LANES), axis=1) * g

    cq = c_ref[:, 0:MLA_Q_LORA].astype(F32)
    ckv = c_ref[:, MLA_Q_LORA:MLA_Q_LORA + MLA_KV_LORA].astype(F32)
    kpe = c_ref[:, MLA_Q_LORA + MLA_KV_LORA:C_WIDTH].astype(F32)
    q = _dot(rms_wide(cq, qag_ref[...]).astype(BF16), wuq_ref[...])
    kv = _dot(rms_wide(ckv, kvag_ref[...]).astype(BF16), wukv_ref[...])
    gq_n, gq_r = gq_ref[:, 0:LANES], gq_ref[:, LANES:2 * LANES]
    gk_n, gk_r = gk_ref[:, 0:LANES], gk_ref[:, LANES:2 * LANES]
    kpe_ss = sumsq(kpe)
    for h in range(MLA_HEADS):
        lo = h * MLA_HEAD_PAD
        qn, qr = q[:, lo:lo + LANES], q[:, lo + LANES:lo + 2 * LANES]
        qh = q[:, lo:lo + 2 * LANES]
        inv = lax.rsqrt(jnp.sum(qh * qh, axis=-1, keepdims=True) * (1.0 / dh) + EPS) * qscale
        qo_ref[:, lo:lo + LANES] = (qn * inv * gq_n).astype(qo_ref.dtype)
        qo_ref[:, lo + LANES:lo + 2 * LANES] = _rope_mxu(
            qr * inv * gq_r, rp, ct, st).astype(qo_ref.dtype)
        kn = kv[:, h * LANES:(h + 1) * LANES]
        inv = lax.rsqrt((jnp.sum(kn * kn, axis=-1, keepdims=True) + kpe_ss) * (1.0 / dh) + EPS)
        ko_ref[lo:lo + LANES, :] = (kn * inv * gk_n).T.astype(ko_ref.dtype)
        ko_ref[lo + LANES:lo + 2 * LANES, :] = _rope_mxu(
            kpe * inv * gk_r, rp, ct, st).T.astype(ko_ref.dtype)
    vo_ref[...] = kv[:, MLA_HEADS * LANES:].astype(vo_ref.dtype)


def _mla_prep(proj, qag, wuq, kvag, wukv, gq, gk, tabs, rp, seq, tm, qscale):
    m = proj.shape[0]
    nrow = seq // tm
    hp = MLA_HEADS * MLA_HEAD_PAD
    tab_spec = pl.BlockSpec((tm, LANES), lambda i: (i % nrow, 0))
    const = lambda shape: pl.BlockSpec(shape, lambda i: (0, 0))
    return pl.pallas_call(
        functools.partial(_mla_prep_kernel, qscale=qscale),
        grid=(m // tm,),
        in_specs=[pl.BlockSpec((tm, C_WIDTH), lambda i: (i, OFF_C // C_WIDTH)),
                  const((1, MLA_Q_LORA)), const((MLA_Q_LORA, hp)),
                  const((1, MLA_KV_LORA)), const((MLA_KV_LORA, 2 * MLA_HEADS * LANES)),
                  const((1, MLA_HEAD_PAD)), const((1, MLA_HEAD_PAD)),
                  tab_spec, tab_spec, const((LANES, LANES))],
        out_specs=[pl.BlockSpec((tm, hp), lambda i: (i, 0)),
                   pl.BlockSpec((None, hp, tm), lambda i: (i // nrow, 0, i % nrow)),
                   pl.BlockSpec((tm, MLA_HEADS * MLA_V), lambda i: (i, 0))],
        out_shape=[jax.ShapeDtypeStruct((m, hp), BF16),
                   jax.ShapeDtypeStruct((m // seq, hp, seq), BF16),
                   jax.ShapeDtypeStruct((m, MLA_HEADS * MLA_V), BF16)],
        compiler_params=_cparams(("parallel",)),
        name="mla_prep",
    )(proj, qag, wuq, kvag, wukv, gq, gk, *tabs, rp)


def _attn_kernel(q_ref, k_ref, v_ref, lam_ref, sg_ref, o_ref, vext, m_scr, acc_scr,
                 *, seq, tk, diff, out_scale):
    tq = q_ref.shape[0]

    @pl.when(pl.program_id(2) == 0)
    def _():
        vext[:, 0:LANES] = v_ref[...]
        vext[:, LANES:2 * LANES] = jnp.ones((seq, LANES), BF16)

    q = q_ref[...]
    if diff:
        lane = lax.broadcasted_iota(jnp.int32, q.shape, 1)
        zero = jnp.zeros_like(q)
        q = jnp.concatenate([jnp.where(lane < DIFF_DH, q, zero),
                             jnp.where(lane >= DIFF_DH, q, zero)], axis=0)
    rows = q.shape[0]
    m_scr[...] = jnp.full((rows, LANES), -1e30, F32)
    acc_scr[...] = jnp.zeros((rows, 2 * LANES), F32)

    def body(j, carry):
        off = pl.multiple_of(j * tk, tk)
        s = _dot(q, k_ref[:, pl.ds(off, tk)])
        m_prev = m_scr[...]
        m_new = jnp.maximum(m_prev, jnp.max(s, axis=1, keepdims=True))
        alpha = jnp.exp2(m_prev - m_new)
        p = jnp.exp2(s - m_new[:, 0:1]).astype(BF16)
        acc_scr[...] = (acc_scr[...] * jnp.concatenate([alpha, alpha], axis=1)
                        + _dot(p, vext[pl.ds(off, tk), :]))
        m_scr[...] = m_new
        return carry

    lax.fori_loop(0, seq // tk, body, 0)

    acc = acc_scr[...]
    o = acc[:, 0:LANES] / acc[:, LANES:2 * LANES]
    if diff:
        o = o[0:tq, :] - lam_ref[...] * o[tq:2 * tq, :]
        o = _rms(o, sg_ref[...]) * out_scale
    o_ref[...] = o.astype(o_ref.dtype)


def _attn_bounded_kernel(q_ref, k_ref, v_ref, lam_ref, sg_ref, o_ref, vext,
                         *, seq, tk, diff, out_scale):
    tq = q_ref.shape[0]

    @pl.when(pl.program_id(2) == 0)
    def _():
        vext[:, 0:LANES] = v_ref[...]
        vext[:, LANES:2 * LANES] = jnp.ones((seq, LANES), BF16)

    q = q_ref[...]
    if diff:
        lane = lax.broadcasted_iota(jnp.int32, q.shape, 1)
        zero = jnp.zeros_like(q)
        q = jnp.concatenate([jnp.where(lane < DIFF_DH, q, zero),
                             jnp.where(lane >= DIFF_DH, q, zero)], axis=0)
    rb = ATTN_ROW_BLOCK

    def softmax_rows(lo):
        acc = None
        for j in range(seq // tk):
            p = jnp.exp2(_dot(q[lo:lo + rb, :], k_ref[:, j * tk:(j + 1) * tk])).astype(BF16)
            pv = _dot(p, vext[j * tk:(j + 1) * tk, :])
            acc = pv if acc is None else acc + pv
        return acc[:, 0:LANES] / acc[:, LANES:2 * LANES]

    for r in range(tq // rb):
        o = softmax_rows(r * rb)
        if diff:
            o = o - lam_ref[...] * softmax_rows(tq + r * rb)
            o = _rms(o, sg_ref[...]) * out_scale
        o_ref[r * rb:(r + 1) * rb, :] = o.astype(o_ref.dtype)


def _score_bound(gq, gk, dh):
    return 1.02 * LOG2E * math.sqrt(dh) * jnp.max(jnp.abs(gq)) * jnp.max(jnp.abs(gk))


def _attention(q, k, v, lam, sg, *, v_off, batch, seq, heads, dq, tq, tk, diff, out_scale, bounded):
    nq = seq // tq
    rows = 2 * tq if diff else tq
    vblk = v_off // LANES
    body = _attn_bounded_kernel if bounded else _attn_kernel
    return pl.pallas_call(
        functools.partial(body, seq=seq, tk=tk, diff=diff, out_scale=out_scale),
        grid=(batch, heads, nq),
        in_specs=[pl.BlockSpec((tq, dq), lambda b, h, i: (b * nq + i, h)),
                  pl.BlockSpec((None, dq, seq), lambda b, h, i: (b, h, 0)),
                  pl.BlockSpec((seq, LANES), lambda b, h, i: (b, vblk + h)),
                  pl.BlockSpec((1, LANES), lambda b, h, i: (0, 0)),
                  pl.BlockSpec((1, LANES), lambda b, h, i: (0, 0))],
        out_specs=pl.BlockSpec((tq, LANES), lambda b, h, i: (b * nq + i, h)),
        out_shape=jax.ShapeDtypeStruct((batch * seq, heads * LANES), BF16),
        scratch_shapes=[pltpu.VMEM((seq, 2 * LANES), BF16)] + ([] if bounded else [
            pltpu.VMEM((rows, LANES), F32), pltpu.VMEM((rows, 2 * LANES), F32)]),
        compiler_params=_cparams(("parallel", "parallel", "arbitrary")),
        name=("diff_attn" if diff else "mla_attn") + ("_bounded" if bounded else "_online"),
    )(q, k, v, lam, sg)


def _retention_kernel(d_ref, lg_ref, gn_ref, c_tab, s_tab, rp_ref,
                      o_ref, qs, kts, o_fwd, o_bwd, state, *, seq):
    ch = RET_CHUNK if seq % RET_CHUNK == 0 else seq
    nc = seq // ch
    ct, st, rp = c_tab[...], s_tab[...], rp_ref[...]
    voff = 2 * LANES
    goff = voff + 2 * RET_DV
    qs[...] = _rope_mxu(d_ref[:, 0:LANES].astype(F32), rp, ct, st)
    kt = (_rope_mxu(d_ref[:, LANES:2 * LANES].astype(F32), rp, ct, st) * (RET_DK ** -0.5)).T
    row = lax.broadcasted_iota(jnp.int32, kt.shape, 0)
    kts[0] = jnp.where(row < RET_DK, kt, 0.0)
    kts[1] = jnp.where(row >= RET_DK, kt, 0.0)
    state[...] = jnp.zeros(state.shape, F32)

    ri = lax.broadcasted_iota(jnp.int32, (ch, ch), 0).astype(F32)
    ci = lax.broadcasted_iota(jnp.int32, (ch, ch), 1).astype(F32)
    qidx = lax.broadcasted_iota(jnp.int32, (ch, 1), 0).astype(F32)
    kidx = lax.broadcasted_iota(jnp.int32, (1, ch), 1).astype(F32)
    dec = []
    for hh in range(2):
        lg = lg_ref[hh:hh + 1, 0:1]
        dec.append(dict(
            intra=jnp.exp(lg * jnp.abs(ri - ci)),
            tail=jnp.exp(lg * (ch - 1.0 - kidx)),
            head=jnp.exp(lg * kidx),
            qf=jnp.exp(lg * (qidx + 1.0)),
            qb=jnp.exp(lg * (ch - qidx)),
            chunk=jnp.exp(lg * float(ch))))

    def sweep(t, carry):
        of = pl.multiple_of(t * ch, ch)
        ob = pl.multiple_of((nc - 1 - t) * ch, ch)
        qf = qs[pl.ds(of, ch), :]
        qb = qs[pl.ds(ob, ch), :]
        for hh in range(2):
            d = dec[hh]
            vlo = hh * RET_DV
            ktf = kts[hh, :, pl.ds(of, ch)]
            vf = d_ref[pl.ds(of, ch), voff + vlo:voff + vlo + RET_DV]
            sc = _dot(qf.astype(BF16), ktf.astype(BF16)) * d["intra"]
            past = state[2 * hh]
            o_fwd[hh, pl.ds(of, ch), :] = (_dot(sc.astype(BF16), vf)
                                           + _dot((qf * d["qf"]).astype(BF16), past.astype(BF16)))
            state[2 * hh] = past * d["chunk"] + _dot((ktf * d["tail"]).astype(BF16), vf)
            ktb = kts[hh, :, pl.ds(ob, ch)]
            vb = d_ref[pl.ds(ob, ch), voff + vlo:voff + vlo + RET_DV]
            fut = state[2 * hh + 1]
            o_bwd[hh, pl.ds(ob, ch), :] = _dot((qb * d["qb"]).astype(BF16), fut.astype(BF16))
            state[2 * hh + 1] = fut * d["chunk"] + _dot((ktb * d["head"]).astype(BF16), vb)
        return carry

    lax.fori_loop(0, nc, sweep, 0, unroll=4 if nc % 4 == 0 else 1)

    lane_mean = jnp.full((RET_DV, RET_DV), 1.0 / RET_DV, BF16)

    def fin(n, carry):
        off = pl.multiple_of(n * ch, ch)
        for hh in range(2):
            vlo = hh * RET_DV
            o = o_fwd[hh, pl.ds(off, ch), :] + o_bwd[hh, pl.ds(off, ch), :]
            oc = o - _dot(o.astype(BF16), lane_mean)
            var = _dot((oc * oc).astype(BF16), lane_mean)
            y = oc * lax.rsqrt(var + EPS) * gn_ref[...]
            g = d_ref[pl.ds(off, ch), goff + vlo:goff + vlo + RET_DV].astype(F32)
            o_ref[pl.ds(off, ch), vlo:vlo + RET_DV] = (g * _sigmoid(g) * y).astype(o_ref.dtype)
        return carry

    lax.fori_loop(0, nc, fin, 0, unroll=2 if nc % 2 == 0 else 1)


def _retention(proj, lg, gn, tabs, rp, batch, seq):
    npair = RET_HEADS // 2
    pw = 2 * RET_DV
    tab_spec = pl.BlockSpec((seq, LANES), lambda b, j: (0, 0))
    return pl.pallas_call(
        functools.partial(_retention_kernel, seq=seq),
        grid=(batch, npair),
        in_specs=[pl.BlockSpec((seq, D_PAIR), lambda b, j: (b, OFF_D // D_PAIR + j)),
                  pl.BlockSpec((None, 2, LANES), lambda b, j: (j, 0, 0)),
                  pl.BlockSpec((1, RET_DV), lambda b, j: (0, 0)),
                  tab_spec, tab_spec, pl.BlockSpec((LANES, LANES), lambda b, j: (0, 0))],
        out_specs=pl.BlockSpec((seq, pw), lambda b, j: (b, j)),
        out_shape=jax.ShapeDtypeStruct((batch * seq, RET_HEADS * RET_DV), BF16),
        scratch_shapes=[pltpu.VMEM((seq, LANES), F32),
                        pltpu.VMEM((2, LANES, seq), F32),
                        pltpu.VMEM((2, seq, RET_DV), F32),
                        pltpu.VMEM((2, seq, RET_DV), F32),
                        pltpu.VMEM((4, LANES, RET_DV), F32)],
        compiler_params=_cparams(("parallel", "parallel")),
        name="retention",
    )(proj, lg, gn, *tabs, rp)


def _merge_kernel(x_ref, gl_ref, gb_ref, ya_ref, yb_ref, yc_ref, yd_ref,
                  wa_ref, wb_ref, wc_ref, wd_ref, wo_ref, o_ref):
    for r in range(x_ref.shape[0] // NORM_ROW_BLOCK):
        rows = slice(r * NORM_ROW_BLOCK, (r + 1) * NORM_ROW_BLOCK)
        merged = None
        for n, (y_ref, w_ref) in enumerate(((ya_ref, wa_ref), (yb_ref, wb_ref),
                                            (yc_ref, wc_ref), (yd_ref, wd_ref))):
            lo = n * D_MODEL
            gate = _sigmoid(gl_ref[rows, lo:lo + D_MODEL].astype(F32)
                            + gb_ref[:, lo:lo + D_MODEL])
            term = gate * _dot(y_ref[rows, :], w_ref[...])
            merged = term if merged is None else merged + term
        o_ref[rows, :] = x_ref[rows, :] + _dot(merged.astype(BF16), wo_ref[...])


def _merge(x2, proj, gb, ys, ws, wo, tm):
    m = x2.shape[0]
    row = lambda width: pl.BlockSpec((tm, width), lambda i: (i, 0))
    wspec = pl.BlockSpec((D_MODEL, D_MODEL), lambda i: (0, 0), pipeline_mode=pl.Buffered(1))
    return pl.pallas_call(
        _merge_kernel,
        grid=(m // tm,),
        in_specs=[row(D_MODEL), row(N_BRANCH * D_MODEL),
                  pl.BlockSpec((1, N_BRANCH * D_MODEL), lambda i: (0, 0)),
                  row(D_MODEL), row(D_MODEL), row(D_MODEL), row(D_MODEL),
                  wspec, wspec, wspec, wspec, wspec],
        out_specs=row(D_MODEL),
        out_shape=jax.ShapeDtypeStruct((m, D_MODEL), F32),
        compiler_params=_cparams(("parallel",)),
        name="merge",
    )(x2, proj, gb, *ys, *ws, wo)


def _ffn_kernel(x_ref, g2_ref, w1_ref, w2_ref, g3_ref, wpg_ref, p_ref, wpe_ref, o_ref,
                acc_ref):
    j = pl.program_id(1)
    blocks = [slice(r * NORM_ROW_BLOCK, (r + 1) * NORM_ROW_BLOCK)
              for r in range(x_ref.shape[0] // NORM_ROW_BLOCK)]

    @pl.when(j == 0)
    def _():
        acc_ref[...] = jnp.zeros_like(acc_ref)

    for rows in blocks:
        xn = _rms(x_ref[rows, :], g2_ref[...]).astype(BF16)
        h = jnp.maximum(_dot(xn, w1_ref[...]), 0.0)
        acc_ref[rows, :] += _dot((h * h).astype(BF16), w2_ref[...])

    @pl.when(j == pl.num_programs(1) - 1)
    def _():
        for rows in blocks:
            x = x_ref[rows, :] + acc_ref[rows, :]
            gate = _sigmoid(_dot(_rms(x, g3_ref[...]).astype(BF16), wpg_ref[...]))
            o_ref[rows, :] = x + gate * _dot(p_ref[rows, :].astype(BF16), wpe_ref[...])


def _ffn(x2, g2, w1, w2, g3, wpg, p2, layer, wpe, tm, tf):
    m = x2.shape[0]
    dff = w1.shape[1]
    pdim = p2.shape[1]
    p_row0 = layer * (m // tm)
    return pl.pallas_call(
        _ffn_kernel,
        grid=(m // tm, dff // tf),
        in_specs=[pl.BlockSpec((tm, D_MODEL), lambda i, j: (i, 0)),
                  pl.BlockSpec((1, D_MODEL), lambda i, j: (0, 0)),
                  pl.BlockSpec((D_MODEL, tf), lambda i, j: (0, j)),
                  pl.BlockSpec((tf, D_MODEL), lambda i, j: (j, 0)),
                  pl.BlockSpec((1, D_MODEL), lambda i, j: (0, 0)),
                  pl.BlockSpec((D_MODEL, D_MODEL), lambda i, j: (0, 0)),
                  pl.BlockSpec((tm, pdim), lambda i, j: (p_row0 + i, 0)),
                  pl.BlockSpec((pdim, D_MODEL), lambda i, j: (0, 0))],
        out_specs=pl.BlockSpec((tm, D_MODEL), lambda i, j: (i, 0)),
        out_shape=jax.ShapeDtypeStruct((m, D_MODEL), F32),
        scratch_shapes=[pltpu.VMEM((tm, D_MODEL), F32)],
        compiler_params=_cparams(("parallel", "arbitrary")),
        name="ffn_ple",
    )(x2, g2, w1, w2, g3, wpg, p2, wpe)


def _block_diag(blocks):
    n, c, d = blocks.shape
    eye = jnp.eye(n, dtype=blocks.dtype)
    return jnp.einsum('ncd,nm->ncmd', blocks, eye).reshape(n * c, n * d)


def _permute_w_in(w_in):
    sizes = (1024, 1024, 1024, 1024, 1024, 384, 256, 64, 512, 512, 1024, 1024, 4096)
    offs = [0]
    for s in sizes:
        offs.append(offs[-1] + s)
    (a_x, a_g, b_q, b_k, b_v, c_q, c_kv, c_kpe, d_q, d_k, d_v, d_g, gates) = [
        w_in[:, offs[n]:offs[n + 1]] for n in range(len(sizes))]
    pad = jnp.zeros((w_in.shape[0], C_WIDTH - 704), w_in.dtype)
    a_tiles = []
    for c in range(D_MODEL // MXU_DIM):
        cols = slice(c * MXU_DIM, (c + 1) * MXU_DIM)
        a_tiles += [a_x[:, cols], a_g[:, cols]]
    d_pairs = []
    for j in range(RET_HEADS // 2):
        qk = slice(2 * j * RET_DK, 2 * (j + 1) * RET_DK)
        vg = slice(2 * j * RET_DV, 2 * (j + 1) * RET_DV)
        d_pairs += [d_q[:, qk], d_k[:, qk], d_v[:, vg], d_g[:, vg]]
    return jnp.concatenate([gates] + a_tiles + [b_q, b_k, b_v] + d_pairs
                           + [c_q, c_kv, c_kpe, pad], axis=1).astype(BF16)


def _lru_weights(wa, ba, wx, bx):
    w = MXU_DIM
    nt = D_MODEL // w
    dense = [_block_diag(m) for m in (wa[0], wx[0], wa[1], wx[1])]
    wd = jnp.stack([jnp.concatenate([d[c * w:(c + 1) * w, c * w:(c + 1) * w] for d in dense], axis=1)
                    for c in range(nt)]).astype(BF16)
    bias = (ba[0], bx[0], ba[1], bx[1])
    bd = jnp.stack([jnp.concatenate([b[c * w:(c + 1) * w] for b in bias])[None, :]
                    for c in range(nt)])
    return wd, bd


def _mla_weights(wuq, wukv, gq, gk):
    dh = MLA_NOPE + MLA_ROPE
    zpad = MLA_HEAD_PAD - dh
    wq = wuq.reshape(MLA_Q_LORA, MLA_HEADS, dh)
    wq = jnp.pad(wq, ((0, 0), (0, 0), (0, zpad))).reshape(MLA_Q_LORA, MLA_HEADS * MLA_HEAD_PAD)
    wkv = wukv.reshape(MLA_KV_LORA, MLA_HEADS, MLA_NOPE + MLA_V)
    wk = wkv[:, :, :MLA_NOPE].reshape(MLA_KV_LORA, MLA_HEADS * MLA_NOPE)
    wv = wkv[:, :, MLA_NOPE:].reshape(MLA_KV_LORA, MLA_HEADS * MLA_V)
    gqp = jnp.pad(gq, (0, zpad))[None, :]
    gkp = jnp.pad(gk, (0, zpad))[None, :]
    return wq.astype(BF16), jnp.concatenate([wk, wv], axis=1).astype(BF16), gqp, gkp


class _Tiles(NamedTuple):
    rows_big: int
    rows_small: int
    q_diff: int
    q_mla: int
    keys: int
    ffn_cols: int


def _tiles(seq):
    return _Tiles(rows_big=min(1024, seq), rows_small=min(512, seq), q_diff=min(2048, seq),
                  q_mla=min(4096, seq), keys=min(512, seq), ffn_cols=2048)


def kernel(x, p, norm1_g, w_in, gate_b, conv_w, conv_b, lru_wa, lru_ba, lru_wx, lru_bx, lru_lambda,
           diff_q_g, diff_k_g, diff_lam, diff_sub_g, mla_qa_g, mla_wuq, mla_kva_g, mla_wukv,
           mla_q_g, mla_k_g, ret_gn_g, w_br_a, w_br_b, w_br_c, w_br_d, w_out, norm2_g,
           w_ff1, w_ff2, norm3_g, w_ple_gate, w_ple_proj):
    batch, seq, d_model = x.shape
    depth = w_in.shape[0]
    assert d_model == D_MODEL and seq % (8 * N_SEG) == 0
    m = batch * seq
    t = _tiles(seq)

    tabs_b = _rope_cs_tables(seq, DIFF_ROT, DIFF_DH, ROPE_THETA)
    tabs_c = _rope_cs_tables(seq, MLA_ROPE, LANES, ROPE_THETA)
    rp_b = _rope_perm(MXU_DIM, DIFF_ROT, DIFF_DH)
    rp_c = _rope_perm(LANES, MLA_ROPE, LANES)
    tabs_d = _rope_cs_tables(seq, RET_DK, RET_DK, RET_THETA)
    rp_d = _rope_perm(LANES, RET_DK, RET_DK)
    log_g = jnp.log1p(-jnp.exp2(-5.0 - jnp.arange(RET_HEADS, dtype=F32)))
    lg = jnp.broadcast_to(log_g.reshape(RET_HEADS // 2, 2, 1), (RET_HEADS // 2, 2, LANES))
    grp = jnp.arange(MXU_DIM) // DIFF_DH
    gm = ((grp[:, None] == grp[None, :]).astype(F32) / DIFF_DH).astype(BF16)
    ones_row = jnp.ones((1, LANES), F32)

    x2 = x.reshape(m, d_model)
    for i in range(depth):
        proj = _inproj(x2, norm1_g[i][None, :], _permute_w_in(w_in[i]), t.rows_big, N_PROJ // 3)

        wd, bd = _lru_weights(lru_wa[i], lru_ba[i], lru_wx[i], lru_bx[i])
        y_a = _rglru(proj, conv_w[i], conv_b[i][None, :], wd, bd, lru_lambda[i], batch, seq)

        lam_init = 0.8 - 0.6 * math.exp(-0.3 * i)
        lp = diff_lam[i].astype(F32)
        lam = jnp.exp(jnp.sum(lp[0] * lp[1])) - jnp.exp(jnp.sum(lp[2] * lp[3])) + lam_init
        gq = jnp.tile(diff_q_g[i], MXU_DIM // DIFF_DH)[None, :] * (DIFF_DH ** -0.5 * LOG2E)
        gk = jnp.tile(diff_k_g[i], MXU_DIM // DIFF_DH)[None, :]
        qb, kb = _diff_prep(proj, gq, gk, tabs_b, gm, rp_b, seq, t.rows_small)
        attn_b = functools.partial(_attention, v_off=OFF_BV, batch=batch, seq=seq, heads=DIFF_HEADS,
                                   dq=LANES, tq=t.q_diff, tk=t.keys, diff=True,
                                   out_scale=1.0 - lam_init)
        y_b = lax.cond(
            _score_bound(diff_q_g[i], diff_k_g[i], DIFF_DH) <= MAX_SCORE_LOG2,
            functools.partial(attn_b, bounded=True), functools.partial(attn_b, bounded=False),
            qb, kb, proj, jnp.full((1, LANES), lam, F32), diff_sub_g[i][None, :])

        wq, wkv, gqp, gkp = _mla_weights(mla_wuq[i], mla_wukv[i], mla_q_g[i], mla_k_g[i])
        qc, kc, vc = _mla_prep(proj, mla_qa_g[i][None, :], wq, mla_kva_g[i][None, :], wkv,
                               gqp, gkp, tabs_c, rp_c, seq, t.rows_small,
                               (MLA_NOPE + MLA_ROPE) ** -0.5 * LOG2E)
        attn_c = functools.partial(_attention, v_off=0, batch=batch, seq=seq, heads=MLA_HEADS,
                                   dq=MLA_HEAD_PAD, tq=t.q_mla, tk=t.keys, diff=False,
                                   out_scale=1.0)
        y_c = lax.cond(
            _score_bound(mla_q_g[i], mla_k_g[i], MLA_NOPE + MLA_ROPE) <= MAX_SCORE_LOG2,
            functools.partial(attn_c, bounded=True), functools.partial(attn_c, bounded=False),
            qc, kc, vc, ones_row, ones_row)

        y_d = _retention(proj, lg, ret_gn_g[i][None, :], tabs_d, rp_d, batch, seq)

        x2 = _merge(x2, proj, gate_b[i].reshape(1, N_BRANCH * D_MODEL), (y_a, y_b, y_c, y_d),
                    tuple(w[i].astype(BF16) for w in (w_br_a, w_br_b, w_br_c, w_br_d)),
                    w_out[i].astype(BF16), t.rows_small)

        x2 = _ffn(x2, norm2_g[i][None, :], w_ff1[i].astype(BF16), w_ff2[i].astype(BF16),
                  norm3_g[i][None, :], w_ple_gate[i].astype(BF16),
                  p.reshape(depth * m, p.shape[-1]), i, w_ple_proj[i].astype(BF16), t.rows_big, t.ffn_cols)
    return x2.reshape(batch, seq, d_model)
```

```python
import functools
import math
from typing import NamedTuple

import jax
import jax.numpy as jnp
from jax import lax
from jax.experimental import pallas as pl
from jax.experimental.pallas import tpu as pltpu

F32 = jnp.float32
BF16 = jnp.bfloat16

EPS = 1e-6
ROPE_THETA = 500000.0
RET_THETA = 10000.0
LRU_C = 8.0
LOG2E = 1.4426950408889634

LANES = 128
MXU_DIM = 256
VMEM_LIMIT = 56 * 1024 * 1024

D_MODEL = 1024
N_BRANCH = 4
LRU_BLOCKS = 16
CONV_WIDTH = 4
DIFF_HEADS = 8
DIFF_DH = 64
DIFF_ROT = 16
MLA_HEADS = 8
MLA_NOPE = 128
MLA_ROPE = 64
MLA_V = 128
MLA_Q_LORA = 384
MLA_KV_LORA = 256
MLA_HEAD_PAD = 256
RET_HEADS = 8
RET_DK = 64
RET_DV = 128
RET_CHUNK = 256
ATTN_ROW_BLOCK = 128
NORM_ROW_BLOCK = 256
MAX_SCORE_LOG2 = 60.0
N_SEG = 8
SEG_PAD = 4

OFF_GATE = 0
OFF_A = 4096
A_TILE = 2 * MXU_DIM
OFF_BQ = 6144
OFF_BK = 7168
OFF_BV = 8192
OFF_D = 9216
D_PAIR = 4 * RET_DK + 4 * RET_DV
OFF_C = 12288
C_WIDTH = 768
N_PROJ = OFF_C + C_WIDTH


def _cparams(sem):
    return pltpu.CompilerParams(dimension_semantics=sem, vmem_limit_bytes=VMEM_LIMIT)


def _sigmoid(x):
    return 1.0 / (1.0 + jnp.exp2(x * -LOG2E))


def _rms(x, g):
    return x * lax.rsqrt(jnp.mean(x * x, axis=-1, keepdims=True) + EPS) * g


def _dot(a, b):
    return jnp.dot(a, b, preferred_element_type=F32)


def _dot_nt(a, b):
    return lax.dot_general(a, b, (((1,), (1,)), ((), ())), preferred_element_type=F32)


def _dot_tn(a, b):
    return lax.dot_general(a, b, (((0,), (0,)), ((), ())), preferred_element_type=F32)


def _inproj_kernel(x_ref, g_ref, w_ref, o_ref):
    for r in range(x_ref.shape[0] // NORM_ROW_BLOCK):
        rows = slice(r * NORM_ROW_BLOCK, (r + 1) * NORM_ROW_BLOCK)
        xn = _rms(x_ref[rows, :], g_ref[...]).astype(BF16)
        o_ref[rows, :] = _dot(xn, w_ref[...]).astype(o_ref.dtype)


def _inproj(x2, g, w, tm, tn):
    m, k = x2.shape
    n = w.shape[1]
    return pl.pallas_call(
        _inproj_kernel,
        grid=(m // tm, n // tn),
        in_specs=[pl.BlockSpec((tm, k), lambda i, j: (i, 0)),
                  pl.BlockSpec((1, k), lambda i, j: (0, 0)),
                  pl.BlockSpec((k, tn), lambda i, j: (0, j))],
        out_specs=pl.BlockSpec((tm, tn), lambda i, j: (i, j)),
        out_shape=jax.ShapeDtypeStruct((m, n), BF16),
        compiler_params=_cparams(("parallel", "arbitrary")),
        name="inproj",
    )(x2, g, w)


def _gelu_tanh(x):
    c = 0.7978845608028654
    return (0.5 * x) * (1.0 + jnp.tanh(x * (c + (c * 0.044715) * (x * x))))


def _rglru_kernel(xg_ref, cw_ref, cb_ref, wd_ref, bd_ref, lam_ref, o_ref,
                  xpad, a0, u0, a1, u1, *, seq):
    w = MXU_DIM
    nslab = w // LANES
    seg = seq // N_SEG
    pitch = seg + SEG_PAD
    left = CONV_WIDTH // 2

    xpad[0:8, :] = jnp.zeros((8, w), F32)
    xpad[seq + 8:seq + 16, :] = jnp.zeros((8, w), F32)
    for s in range(N_SEG):
        xpad[8 + s * seg:8 + (s + 1) * seg, :] = xg_ref[s * seg:(s + 1) * seg, 0:w].astype(F32)

    lam = lam_ref[...]
    nlam = -lam
    nls = (-LRU_C * LOG2E) * (jnp.maximum(nlam, 0.0) + jnp.log(1.0 + jnp.exp(-jnp.abs(nlam))))

    a_scr = (a0, a1)
    u_scr = (u0, u1)
    for s in range(N_SEG):
        base = 8 + s * seg - left
        xc = cb_ref[...]
        for t in range(CONV_WIDTH):
            xc = xc + xpad[base + t:base + t + seg, :] * cw_ref[t:t + 1, :]
        z = _dot(xc.astype(BF16), wd_ref[0]) + bd_ref[0]
        for d in range(2):
            r = _sigmoid(z[:, (2 * d) * w:(2 * d + 1) * w])
            gi = _sigmoid(z[:, (2 * d + 1) * w:(2 * d + 2) * w])
            a = jnp.exp2(nls[d:d + 1, :] * r)
            v = 1.0 - a * a
            u = (v * lax.rsqrt(jnp.maximum(v, 1e-30))) * (gi * xc)
            for sl in range(nslab):
                a_scr[d][sl, s * pitch:s * pitch + seg, :] = a[:, sl * LANES:(sl + 1) * LANES]
                u_scr[d][sl, s * pitch:s * pitch + seg, :] = u[:, sl * LANES:(sl + 1) * LANES]

    def rows(t):
        return pl.ds(t, N_SEG, stride=pitch)

    def scan_body(t, carry):
        new = []
        tb = seg - 1 - t
        for sl in range(nslab):
            hf, pf, hb, pb = carry[sl]
            af = a0[sl, rows(t), :]
            hf = af * hf + u0[sl, rows(t), :]
            pf = pf * af
            u0[sl, rows(t), :] = hf
            a0[sl, rows(t), :] = pf
            ab = a1[sl, rows(tb), :]
            hb = ab * hb + u1[sl, rows(tb), :]
            pb = pb * ab
            u1[sl, rows(tb), :] = hb
            a1[sl, rows(tb), :] = pb
            new.append((hf, pf, hb, pb))
        return tuple(new)

    zero = jnp.zeros((N_SEG, LANES), F32)
    one = jnp.ones((N_SEG, LANES), F32)
    fin = lax.fori_loop(0, seg, scan_body, tuple((zero, one, zero, one) for _ in range(nslab)),
                        unroll=4)

    cf, cb = [], []
    for sl in range(nslab):
        hf, pf, hb, pb = fin[sl]
        c = jnp.zeros((1, LANES), F32)
        cfr = [c]
        for s in range(1, N_SEG):
            c = pf[s - 1:s, :] * c + hf[s - 1:s, :]
            cfr.append(c)
        cf.append(jnp.concatenate(cfr, axis=0))
        c = jnp.zeros((1, LANES), F32)
        cbr = [c]
        for s in range(N_SEG - 2, -1, -1):
            c = pb[s + 1:s + 2, :] * c + hb[s + 1:s + 2, :]
            cbr.append(c)
        cb.append(jnp.concatenate(cbr[::-1], axis=0))

    def fix_body(t, carry):
        for sl in range(nslab):
            hf = u0[sl, rows(t), :] + a0[sl, rows(t), :] * cf[sl]
            hb = u1[sl, rows(t), :] + a1[sl, rows(t), :] * cb[sl]
            u0[sl, rows(t), :] = hf + hb
        return carry

    lax.fori_loop(0, seg, fix_body, 0, unroll=4)

    for s in range(N_SEG):
        g = _gelu_tanh(xg_ref[s * seg:(s + 1) * seg, w:2 * w].astype(F32))
        for sl in range(nslab):
            h = u0[sl, s * pitch:s * pitch + seg, :]
            o_ref[s * seg:(s + 1) * seg, sl * LANES:(sl + 1) * LANES] = (
                h * g[:, sl * LANES:(sl + 1) * LANES]).astype(o_ref.dtype)


def _rglru(proj, cw, cb, wd, bd, lam, batch, seq):
    w = MXU_DIM
    nt = D_MODEL // w
    nslab = w // LANES
    rows = N_SEG * (seq // N_SEG + SEG_PAD)
    scan_scratch = pltpu.VMEM((nslab, rows, LANES), F32)
    return pl.pallas_call(
        functools.partial(_rglru_kernel, seq=seq),
        grid=(batch, nt),
        in_specs=[pl.BlockSpec((seq, A_TILE), lambda b, c: (b, OFF_A // A_TILE + c)),
                  pl.BlockSpec((CONV_WIDTH, w), lambda b, c: (0, c)),
                  pl.BlockSpec((1, w), lambda b, c: (0, c)),
                  pl.BlockSpec((1, w, 4 * w), lambda b, c: (c, 0, 0)),
                  pl.BlockSpec((1, 1, 4 * w), lambda b, c: (c, 0, 0)),
                  pl.BlockSpec((2, w), lambda b, c: (0, c))],
        out_specs=pl.BlockSpec((seq, w), lambda b, c: (b, c)),
        out_shape=jax.ShapeDtypeStruct((batch * seq, D_MODEL), BF16),
        scratch_shapes=[pltpu.VMEM((seq + 16, w), F32),
                        scan_scratch, scan_scratch, scan_scratch, scan_scratch],
        compiler_params=_cparams(("parallel", "parallel")),
        name="rglru",
    )(proj, cw, cb, wd, bd, lam)


def _rope_mxu(x, perm, c, s):
    return x * c + _dot(x.astype(BF16), perm) * s


def _rope_perm(width, rot, period):
    half = rot // 2
    lane = jnp.arange(width)
    j = lane % period
    src = jnp.where(j < half, lane + half, lane - half)
    sign = jnp.where(j < half, -1.0, 1.0) * (j < rot)
    return ((lane[:, None] == src[None, :]) * sign[None, :]).astype(BF16)


def _rope_cs_tables(seq, rot, period, theta):
    pos = jnp.arange(seq, dtype=F32)
    inv_freq = theta ** (-jnp.arange(0, rot, 2, dtype=F32) / rot)
    ang = pos[:, None] * inv_freq[None, :]
    cos, sin = jnp.cos(ang), jnp.sin(ang)
    pad = period - rot
    c = jnp.concatenate([cos, cos, jnp.ones((seq, pad), F32)], axis=1)
    s = jnp.concatenate([sin, sin, jnp.zeros((seq, pad), F32)], axis=1)
    rep = LANES // period
    return jnp.tile(c, (1, rep)), jnp.tile(s, (1, rep))


def _diff_prep_kernel(q_ref, k_ref, gq_ref, gk_ref, c_ref, s_ref, gm_ref, rp_ref,
                      qo_ref, ko_ref):
    c = jnp.concatenate([c_ref[...]] * (MXU_DIM // LANES), axis=1)
    sn = jnp.concatenate([s_ref[...]] * (MXU_DIM // LANES), axis=1)
    gm = gm_ref[...]
    rp = rp_ref[...]
    for src, g_ref, is_key in ((q_ref, gq_ref, False), (k_ref, gk_ref, True)):
        for blk in range(D_MODEL // MXU_DIM):
            x = src[:, blk * MXU_DIM:(blk + 1) * MXU_DIM].astype(F32)
            ms = _dot((x * x).astype(BF16), gm)
            xr = _rope_mxu(x * lax.rsqrt(ms + EPS) * g_ref[...], rp, c, sn)
            for h in range(MXU_DIM // LANES):
                xb = xr[:, h * LANES:(h + 1) * LANES]
                lo = blk * MXU_DIM + h * LANES
                if is_key:
                    ko_ref[lo:lo + LANES, :] = xb.T.astype(ko_ref.dtype)
                else:
                    qo_ref[:, lo:lo + LANES] = xb.astype(qo_ref.dtype)


def _diff_prep(proj, gq, gk, tabs, gm, rp, seq, tm):
    m = proj.shape[0]
    nrow = seq // tm
    tab_spec = pl.BlockSpec((tm, LANES), lambda i: (i % nrow, 0))
    return pl.pallas_call(
        _diff_prep_kernel,
        grid=(m // tm,),
        in_specs=[pl.BlockSpec((tm, D_MODEL), lambda i: (i, OFF_BQ // D_MODEL)),
                  pl.BlockSpec((tm, D_MODEL), lambda i: (i, OFF_BK // D_MODEL)),
                  pl.BlockSpec((1, MXU_DIM), lambda i: (0, 0)),
                  pl.BlockSpec((1, MXU_DIM), lambda i: (0, 0)),
                  tab_spec, tab_spec,
                  pl.BlockSpec((MXU_DIM, MXU_DIM), lambda i: (0, 0)),
                  pl.BlockSpec((MXU_DIM, MXU_DIM), lambda i: (0, 0))],
        out_specs=[pl.BlockSpec((tm, D_MODEL), lambda i: (i, 0)),
                   pl.BlockSpec((None, D_MODEL, tm), lambda i: (i // nrow, 0, i % nrow))],
        out_shape=[jax.ShapeDtypeStruct((m, D_MODEL), BF16),
                   jax.ShapeDtypeStruct((m // seq, D_MODEL, seq), BF16)],
        compiler_params=_cparams(("parallel",)),
        name="diff_prep",
    )(proj, proj, gq, gk, *tabs, gm, rp)


def _mla_prep_kernel(c_ref, qag_ref, wuq_ref, kvag_ref, wukv_ref, gq_ref, gk_ref,
                     c_tab, s_tab, rp_ref, qo_ref, ko_ref, vo_ref, *, qscale):
    dh = MLA_NOPE + MLA_ROPE
    ct, st, rp = c_tab[...], s_tab[...], rp_ref[...]

    def sumsq(x):
        return _dot((x * x).astype(BF16), jnp.ones((x.shape[1], LANES), BF16))

    def rms_wide(x, g):
        inv = lax.rsqrt(sumsq(x) * (1.0 / x.shape[1]) + EPS)
        return x * jnp.concatenate([inv] * (x.shape[1] // LANES), axis=1) * g

    cq = c_ref[:, 0:MLA_Q_LORA].astype(F32)
    ckv = c_ref[:, MLA_Q_LORA:MLA_Q_LORA + MLA_KV_LORA].astype(F32)
    kpe = c_ref[:, MLA_Q_LORA + MLA_KV_LORA:C_WIDTH].astype(F32)
    q = _dot(rms_wide(cq, qag_ref[...]).astype(BF16), wuq_ref[...])
    kv = _dot(rms_wide(ckv, kvag_ref[...]).astype(BF16), wukv_ref[...])
    gq_n, gq_r = gq_ref[:, 0:LANES], gq_ref[:, LANES:2 * LANES]
    gk_n, gk_r = gk_ref[:, 0:LANES], gk_ref[:, LANES:2 * LANES]
    kpe_ss = sumsq(kpe)
    for h in range(MLA_HEADS):
        lo = h * MLA_HEAD_PAD
        qn, qr = q[:, lo:lo + LANES], q[:, lo + LANES:lo + 2 * LANES]
        qh = q[:, lo:lo + 2 * LANES]
        inv = lax.rsqrt(jnp.sum(qh * qh, axis=-1, keepdims=True) * (1.0 / dh) + EPS) * qscale
        qo_ref[:, lo:lo + LANES] = (qn * inv * gq_n).astype(qo_ref.dtype)
        qo_ref[:, lo + LANES:lo + 2 * LANES] = _rope_mxu(
            qr * inv * gq_r, rp, ct, st).astype(qo_ref.dtype)
        kn = kv[:, h * LANES:(h + 1) * LANES]
        inv = lax.rsqrt((jnp.sum(kn * kn, axis=-1, keepdims=True) + kpe_ss) * (1.0 / dh) + EPS)
        ko_ref[lo:lo + LANES, :] = (kn * inv * gk_n).T.astype(ko_ref.dtype)
        ko_ref[lo + LANES:lo + 2 * LANES, :] = _rope_mxu(
            kpe * inv * gk_r, rp, ct, st).T.astype(ko_ref.dtype)
    vo_ref[...] = kv[:, MLA_HEADS * LANES:].astype(vo_ref.dtype)


def _mla_prep(proj, qag, wuq, kvag, wukv, gq, gk, tabs, rp, seq, tm, qscale):
    m = proj.shape[0]
    nrow = seq // tm
    hp = MLA_HEADS * MLA_HEAD_PAD
    tab_spec = pl.BlockSpec((tm, LANES), lambda i: (i % nrow, 0))
    const = lambda shape: pl.BlockSpec(shape, lambda i: (0, 0))
    return pl.pallas_call(
        functools.partial(_mla_prep_kernel, qscale=qscale),
        grid=(m // tm,),
        in_specs=[pl.BlockSpec((tm, C_WIDTH), lambda i: (i, OFF_C // C_WIDTH)),
                  const((1, MLA_Q_LORA)), const((MLA_Q_LORA, hp)),
                  const((1, MLA_KV_LORA)), const((MLA_KV_LORA, 2 * MLA_HEADS * LANES)),
                  const((1, MLA_HEAD_PAD)), const((1, MLA_HEAD_PAD)),
                  tab_spec, tab_spec, const((LANES, LANES))],
        out_specs=[pl.BlockSpec((tm, hp), lambda i: (i, 0)),
                   pl.BlockSpec((None, hp, tm), lambda i: (i // nrow, 0, i % nrow)),
                   pl.BlockSpec((tm, MLA_HEADS * MLA_V), lambda i: (i, 0))],
        out_shape=[jax.ShapeDtypeStruct((m, hp), BF16),
                   jax.ShapeDtypeStruct((m // seq, hp, seq), BF16),
                   jax.ShapeDtypeStruct((m, MLA_HEADS * MLA_V), BF16)],
        compiler_params=_cparams(("parallel",)),
        name="mla_prep",
    )(proj, qag, wuq, kvag, wukv, gq, gk, *tabs, rp)


def _attn_kernel(q_ref, k_ref, v_ref, lam_ref, sg_ref, o_ref, vext, m_scr, acc_scr,
                 *, seq, tk, diff, out_scale):
    tq = q_ref.shape[0]

    @pl.when(pl.program_id(2) == 0)
    def _():
        vext[:, 0:LANES] = v_ref[...]
        vext[:, LANES:2 * LANES] = jnp.ones((seq, LANES), BF16)

    q = q_ref[...]
    if diff:
        lane = lax.broadcasted_iota(jnp.int32, q.shape, 1)
        zero = jnp.zeros_like(q)
        q = jnp.concatenate([jnp.where(lane < DIFF_DH, q, zero),
                             jnp.where(lane >= DIFF_DH, q, zero)], axis=0)
    rows = q.shape[0]
    m_scr[...] = jnp.full((rows, LANES), -1e30, F32)
    acc_scr[...] = jnp.zeros((rows, 2 * LANES), F32)

    def body(j, carry):
        off = pl.multiple_of(j * tk, tk)
        s = _dot(q, k_ref[:, pl.ds(off, tk)])
        m_prev = m_scr[...]
        m_new = jnp.maximum(m_prev, jnp.max(s, axis=1, keepdims=True))
        alpha = jnp.exp2(m_prev - m_new)
        p = jnp.exp2(s - m_new[:, 0:1]).astype(BF16)
        acc_scr[...] = (acc_scr[...] * jnp.concatenate([alpha, alpha], axis=1)
                        + _dot(p, vext[pl.ds(off, tk), :]))
        m_scr[...] = m_new
        return carry

    lax.fori_loop(0, seq // tk, body, 0)

    acc = acc_scr[...]
    o = acc[:, 0:LANES] / acc[:, LANES:2 * LANES]
    if diff:
        o = o[0:tq, :] - lam_ref[...] * o[tq:2 * tq, :]
        o = _rms(o, sg_ref[...]) * out_scale
    o_ref[...] = o.astype(o_ref.dtype)


def _attn_bounded_kernel(q_ref, k_ref, v_ref, lam_ref, sg_ref, o_ref, vext,
                         *, seq, tk, diff, out_scale):
    tq = q_ref.shape[0]

    @pl.when(pl.program_id(2) == 0)
    def _():
        vext[:, 0:LANES] = v_ref[...]
        vext[:, LANES:2 * LANES] = jnp.ones((seq, LANES), BF16)

    q = q_ref[...]
    if diff:
        lane = lax.broadcasted_iota(jnp.int32, q.shape, 1)
        zero = jnp.zeros_like(q)
        q = jnp.concatenate([jnp.where(lane < DIFF_DH, q, zero),
                             jnp.where(lane >= DIFF_DH, q, zero)], axis=0)
    rb = ATTN_ROW_BLOCK

    def softmax_rows(lo):
        acc = None
        for j in range(seq // tk):
            p = jnp.exp2(_dot(q[lo:lo + rb, :], k_ref[:, j * tk:(j + 1) * tk])).astype(BF16)
            pv = _dot(p, vext[j * tk:(j + 1) * tk, :])
            acc = pv if acc is None else acc + pv
        return acc[:, 0:LANES] / acc[:, LANES:2 * LANES]

    for r in range(tq // rb):
        o = softmax_rows(r * rb)
        if diff:
            o = o - lam_ref[...] * softmax_rows(tq + r * rb)
            o = _rms(o, sg_ref[...]) * out_scale
        o_ref[r * rb:(r + 1) * rb, :] = o.astype(o_ref.dtype)


def _score_bound(gq, gk, dh):
    return 1.02 * LOG2E * math.sqrt(dh) * jnp.max(jnp.abs(gq)) * jnp.max(jnp.abs(gk))


def _attention(q, k, v, lam, sg, *, v_off, batch, seq, heads, dq, tq, tk, diff, out_scale, bounded):
    nq = seq // tq
    rows = 2 * tq if diff else tq
    vblk = v_off // LANES
    body = _attn_bounded_kernel if bounded else _attn_kernel
    return pl.pallas_call(
        functools.partial(body, seq=seq, tk=tk, diff=diff, out_scale=out_scale),
        grid=(batch, heads, nq),
        in_specs=[pl.BlockSpec((tq, dq), lambda b, h, i: (b * nq + i, h)),
                  pl.BlockSpec((None, dq, seq), lambda b, h, i: (b, h, 0)),
                  pl.BlockSpec((seq, LANES), lambda b, h, i: (b, vblk + h)),
                  pl.BlockSpec((1, LANES), lambda b, h, i: (0, 0)),
                  pl.BlockSpec((1, LANES), lambda b, h, i: (0, 0))],
        out_specs=pl.BlockSpec((tq, LANES), lambda b, h, i: (b * nq + i, h)),
        out_shape=jax.ShapeDtypeStruct((batch * seq, heads * LANES), BF16),
        scratch_shapes=[pltpu.VMEM((seq, 2 * LANES), BF16)] + ([] if bounded else [
            pltpu.VMEM((rows, LANES), F32), pltpu.VMEM((rows, 2 * LANES), F32)]),
        compiler_params=_cparams(("parallel", "parallel", "arbitrary")),
        name=("diff_attn" if diff else "mla_attn") + ("_bounded" if bounded else "_online"),
    )(q, k, v, lam, sg)


def _retention_kernel(d_ref, lg_ref, gn_ref, c_tab, s_tab, rp_ref,
                      o_ref, qs, kts, o_fwd, o_bwd, state, *, seq):
    ch = RET_CHUNK if seq % RET_CHUNK == 0 else seq
    nc = seq // ch
    ct, st, rp = c_tab[...], s_tab[...], rp_ref[...]
    voff = 2 * LANES
    goff = voff + 2 * RET_DV
    qs[...] = _rope_mxu(d_ref[:, 0:LANES].astype(F32), rp, ct, st)
    kt = (_rope_mxu(d_ref[:, LANES:2 * LANES].astype(F32), rp, ct, st) * (RET_DK ** -0.5)).T
    row = lax.broadcasted_iota(jnp.int32, kt.shape, 0)
    kts[0] = jnp.where(row < RET_DK, kt, 0.0)
    kts[1] = jnp.where(row >= RET_DK, kt, 0.0)
    state[...] = jnp.zeros(state.shape, F32)

    ri = lax.broadcasted_iota(jnp.int32, (ch, ch), 0).astype(F32)
    ci = lax.broadcasted_iota(jnp.int32, (ch, ch), 1).astype(F32)
    qidx = lax.broadcasted_iota(jnp.int32, (ch, 1), 0).astype(F32)
    kidx = lax.broadcasted_iota(jnp.int32, (1, ch), 1).astype(F32)
    dec = []
    for hh in range(2):
        lg = lg_ref[hh:hh + 1, 0:1]
        dec.append(dict(
            intra=jnp.exp(lg * jnp.abs(ri - ci)),
            tail=jnp.exp(lg * (ch - 1.0 - kidx)),
            head=jnp.exp(lg * kidx),
            qf=jnp.exp(lg * (qidx + 1.0)),
            qb=jnp.exp(lg * (ch - qidx)),
            chunk=jnp.exp(lg * float(ch))))

    def sweep(t, carry):
        of = pl.multiple_of(t * ch, ch)
        ob = pl.multiple_of((nc - 1 - t) * ch, ch)
        qf = qs[pl.ds(of, ch), :]
        qb = qs[pl.ds(ob, ch), :]
        for hh in range(2):
            d = dec[hh]
            vlo = hh * RET_DV
            ktf = kts[hh, :, pl.ds(of, ch)]
            vf = d_ref[pl.ds(of, ch), voff + vlo:voff + vlo + RET_DV]
            sc = _dot(qf.astype(BF16), ktf.astype(BF16)) * d["intra"]
            past = state[2 * hh]
            o_fwd[hh, pl.ds(of, ch), :] = (_dot(sc.astype(BF16), vf)
                                           + _dot((qf * d["qf"]).astype(BF16), past.astype(BF16)))
            state[2 * hh] = past * d["chunk"] + _dot((ktf * d["tail"]).astype(BF16), vf)
            ktb = kts[hh, :, pl.ds(ob, ch)]
            vb = d_ref[pl.ds(ob, ch), voff + vlo:voff + vlo + RET_DV]
            fut = state[2 * hh + 1]
            o_bwd[hh, pl.ds(ob, ch), :] = _dot((qb * d["qb"]).astype(BF16), fut.astype(BF16))
            state[2 * hh + 1] = fut * d["chunk"] + _dot((ktb * d["head"]).astype(BF16), vb)
        return carry

    lax.fori_loop(0, nc, sweep, 0, unroll=8 if nc % 8 == 0 else 1)

    lane_mean = jnp.full((RET_DV, RET_DV), 1.0 / RET_DV, BF16)

    def fin(n, carry):
        off = pl.multiple_of(n * ch, ch)
        for hh in range(2):
            vlo = hh * RET_DV
            o = o_fwd[hh, pl.ds(off, ch), :] + o_bwd[hh, pl.ds(off, ch), :]
            oc = o - _dot(o.astype(BF16), lane_mean)
            var = _dot((oc * oc).astype(BF16), lane_mean)
            y = oc * lax.rsqrt(var + EPS) * gn_ref[...]
            g = d_ref[pl.ds(off, ch), goff + vlo:goff + vlo + RET_DV].astype(F32)
            o_ref[pl.ds(off, ch), vlo:vlo + RET_DV] = (g * _sigmoid(g) * y).astype(o_ref.dtype)
        return carry

    lax.fori_loop(0, nc, fin, 0, unroll=4 if nc % 4 == 0 else 1)


def _retention(proj, lg, gn, tabs, rp, batch, seq):
    npair = RET_HEADS // 2
    pw = 2 * RET_DV
    tab_spec = pl.BlockSpec((seq, LANES), lambda b, j: (0, 0))
    return pl.pallas_call(
        functools.partial(_retention_kernel, seq=seq),
        grid=(batch, npair),
        in_specs=[pl.BlockSpec((seq, D_PAIR), lambda b, j: (b, OFF_D // D_PAIR + j)),
                  pl.BlockSpec((None, 2, LANES), lambda b, j: (j, 0, 0)),
                  pl.BlockSpec((1, RET_DV), lambda b, j: (0, 0)),
                  tab_spec, tab_spec, pl.BlockSpec((LANES, LANES), lambda b, j: (0, 0))],
        out_specs=pl.BlockSpec((seq, pw), lambda b, j: (b, j)),
        out_shape=jax.ShapeDtypeStruct((batch * seq, RET_HEADS * RET_DV), BF16),
        scratch_shapes=[pltpu.VMEM((seq, LANES), F32),
                        pltpu.VMEM((2, LANES, seq), F32),
                        pltpu.VMEM((2, seq, RET_DV), F32),
                        pltpu.VMEM((2, seq, RET_DV), F32),
                        pltpu.VMEM((4, LANES, RET_DV), F32)],
        compiler_params=_cparams(("parallel", "parallel")),
        name="retention",
    )(proj, lg, gn, *tabs, rp)


def _merge_kernel(x_ref, gl_ref, gb_ref, ya_ref, yb_ref, yc_ref, yd_ref,
                  wa_ref, wb_ref, wc_ref, wd_ref, wo_ref, o_ref):
    for r in range(x_ref.shape[0] // NORM_ROW_BLOCK):
        rows = slice(r * NORM_ROW_BLOCK, (r + 1) * NORM_ROW_BLOCK)
        merged = None
        for n, (y_ref, w_ref) in enumerate(((ya_ref, wa_ref), (yb_ref, wb_ref),
                                            (yc_ref, wc_ref), (yd_ref, wd_ref))):
            lo = n * D_MODEL
            gate = _sigmoid(gl_ref[rows, lo:lo + D_MODEL].astype(F32)
                            + gb_ref[:, lo:lo + D_MODEL])
            term = gate * _dot(y_ref[rows, :], w_ref[...])
            merged = term if merged is None else merged + term
        o_ref[rows, :] = x_ref[rows, :] + _dot(merged.astype(BF16), wo_ref[...])


def _merge(x2, proj, gb, ys, ws, wo, tm):
    m = x2.shape[0]
    row = lambda width: pl.BlockSpec((tm, width), lambda i: (i, 0))
    wspec = pl.BlockSpec((D_MODEL, D_MODEL), lambda i: (0, 0), pipeline_mode=pl.Buffered(1))
    return pl.pallas_call(
        _merge_kernel,
        grid=(m // tm,),
        in_specs=[row(D_MODEL), row(N_BRANCH * D_MODEL),
                  pl.BlockSpec((1, N_BRANCH * D_MODEL), lambda i: (0, 0)),
                  row(D_MODEL), row(D_MODEL), row(D_MODEL), row(D_MODEL),
                  wspec, wspec, wspec, wspec, wspec],
        out_specs=row(D_MODEL),
        out_shape=jax.ShapeDtypeStruct((m, D_MODEL), F32),
        compiler_params=_cparams(("parallel",)),
        name="merge",
    )(x2, proj, gb, *ys, *ws, wo)


def _ffn_kernel(x_ref, g2_ref, w1_ref, w2_ref, g3_ref, wpg_ref, p_ref, wpe_ref, o_ref,
                acc_ref):
    j = pl.program_id(1)
    blocks = [slice(r * NORM_ROW_BLOCK, (r + 1) * NORM_ROW_BLOCK)
              for r in range(x_ref.shape[0] // NORM_ROW_BLOCK)]

    @pl.when(j == 0)
    def _():
        acc_ref[...] = jnp.zeros_like(acc_ref)

    for rows in blocks:
        xn = _rms(x_ref[rows, :], g2_ref[...]).astype(BF16)
        h = jnp.maximum(_dot(xn, w1_ref[...]), 0.0)
        acc_ref[rows, :] += _dot((h * h).astype(BF16), w2_ref[...])

    @pl.when(j == pl.num_programs(1) - 1)
    def _():
        for rows in blocks:
            x = x_ref[rows, :] + acc_ref[rows, :]
            gate = _sigmoid(_dot(_rms(x, g3_ref[...]).astype(BF16), wpg_ref[...]))
            o_ref[rows, :] = x + gate * _dot(p_ref[rows, :].astype(BF16), wpe_ref[...])


def _ffn(x2, g2, w1, w2, g3, wpg, p2, layer, wpe, tm, tf):
    m = x2.shape[0]
    dff = w1.shape[1]
    pdim = p2.shape[1]
    p_row0 = layer * (m // tm)
    return pl.pallas_call(
        _ffn_kernel,
        grid=(m // tm, dff // tf),
        in_specs=[pl.BlockSpec((tm, D_MODEL), lambda i, j: (i, 0)),
                  pl.BlockSpec((1, D_MODEL), lambda i, j: (0, 0)),
                  pl.BlockSpec((D_MODEL, tf), lambda i, j: (0, j)),
                  pl.BlockSpec((tf, D_MODEL), lambda i, j: (j, 0)),
                  pl.BlockSpec((1, D_MODEL), lambda i, j: (0, 0)),
                  pl.BlockSpec((D_MODEL, D_MODEL), lambda i, j: (0, 0)),
                  pl.BlockSpec((tm, pdim), lambda i, j: (p_row0 + i, 0)),
                  pl.BlockSpec((pdim, D_MODEL), lambda i, j: (0, 0))],
        out_specs=pl.BlockSpec((tm, D_MODEL), lambda i, j: (i, 0)),
        out_shape=jax.ShapeDtypeStruct((m, D_MODEL), F32),
        scratch_shapes=[pltpu.VMEM((tm, D_MODEL), F32)],
        compiler_params=_cparams(("parallel", "arbitrary")),
        name="ffn_ple",
    )(x2, g2, w1, w2, g3, wpg, p2, wpe)


def _block_diag(blocks):
    n, c, d = blocks.shape
    eye = jnp.eye(n, dtype=blocks.dtype)
    return jnp.einsum('ncd,nm->ncmd', blocks, eye).reshape(n * c, n * d)


def _permute_w_in(w_in):
    sizes = (1024, 1024, 1024, 1024, 1024, 384, 256, 64, 512, 512, 1024, 1024, 4096)
    offs = [0]
    for s in sizes:
        offs.append(offs[-1] + s)
    (a_x, a_g, b_q, b_k, b_v, c_q, c_kv, c_kpe, d_q, d_k, d_v, d_g, gates) = [
        w_in[:, offs[n]:offs[n + 1]] for n in range(len(sizes))]
    pad = jnp.zeros((w_in.shape[0], C_WIDTH - 704), w_in.dtype)
    a_tiles = []
    for c in range(D_MODEL // MXU_DIM):
        cols = slice(c * MXU_DIM, (c + 1) * MXU_DIM)
        a_tiles += [a_x[:, cols], a_g[:, cols]]
    d_pairs = []
    for j in range(RET_HEADS // 2):
        qk = slice(2 * j * RET_DK, 2 * (j + 1) * RET_DK)
        vg = slice(2 * j * RET_DV, 2 * (j + 1) * RET_DV)
        d_pairs += [d_q[:, qk], d_k[:, qk], d_v[:, vg], d_g[:, vg]]
    return jnp.concatenate([gates] + a_tiles + [b_q, b_k, b_v] + d_pairs
                           + [c_q, c_kv, c_kpe, pad], axis=1).astype(BF16)


def _lru_weights(wa, ba, wx, bx):
    w = MXU_DIM
    nt = D_MODEL // w
    dense = [_block_diag(m) for m in (wa[0], wx[0], wa[1], wx[1])]
    wd = jnp.stack([jnp.concatenate([d[c * w:(c + 1) * w, c * w:(c + 1) * w] for d in dense], axis=1)
                    for c in range(nt)]).astype(BF16)
    bias = (ba[0], bx[0], ba[1], bx[1])
    bd = jnp.stack([jnp.concatenate([b[c * w:(c + 1) * w] for b in bias])[None, :]
                    for c in range(nt)])
    return wd, bd


def _mla_weights(wuq, wukv, gq, gk):
    dh = MLA_NOPE + MLA_ROPE
    zpad = MLA_HEAD_PAD - dh
    wq = wuq.reshape(MLA_Q_LORA, MLA_HEADS, dh)
    wq = jnp.pad(wq, ((0, 0), (0, 0), (0, zpad))).reshape(MLA_Q_LORA, MLA_HEADS * MLA_HEAD_PAD)
    wkv = wukv.reshape(MLA_KV_LORA, MLA_HEADS, MLA_NOPE + MLA_V)
    wk = wkv[:, :, :MLA_NOPE].reshape(MLA_KV_LORA, MLA_HEADS * MLA_NOPE)
    wv = wkv[:, :, MLA_NOPE:].reshape(MLA_KV_LORA, MLA_HEADS * MLA_V)
    gqp = jnp.pad(gq, (0, zpad))[None, :]
    gkp = jnp.pad(gk, (0, zpad))[None, :]
    return wq.astype(BF16), jnp.concatenate([wk, wv], axis=1).astype(BF16), gqp, gkp


class _Tiles(NamedTuple):
    rows_big: int
    rows_small: int
    q_diff: int
    q_mla: int
    keys: int
    ffn_cols: int


def _tiles(seq):
    return _Tiles(rows_big=min(1024, seq), rows_small=min(512, seq), q_diff=min(2048, seq),
                  q_mla=min(4096, seq), keys=min(512, seq), ffn_cols=2048)


def kernel(x, p, norm1_g, w_in, gate_b, conv_w, conv_b, lru_wa, lru_ba, lru_wx, lru_bx, lru_lambda,
           diff_q_g, diff_k_g, diff_lam, diff_sub_g, mla_qa_g, mla_wuq, mla_kva_g, mla_wukv,
           mla_q_g, mla_k_g, ret_gn_g, w_br_a, w_br_b, w_br_c, w_br_d, w_out, norm2_g,
           w_ff1, w_ff2, norm3_g, w_ple_gate, w_ple_proj):
    batch, seq, d_model = x.shape
    depth = w_in.shape[0]
    assert d_model == D_MODEL and seq % (8 * N_SEG) == 0
    m = batch * seq
    t = _tiles(seq)

    tabs_b = _rope_cs_tables(seq, DIFF_ROT, DIFF_DH, ROPE_THETA)
    tabs_c = _rope_cs_tables(seq, MLA_ROPE, LANES, ROPE_THETA)
    rp_b = _rope_perm(MXU_DIM, DIFF_ROT, DIFF_DH)
    rp_c = _rope_perm(LANES, MLA_ROPE, LANES)
    tabs_d = _rope_cs_tables(seq, RET_DK, RET_DK, RET_THETA)
    rp_d = _rope_perm(LANES, RET_DK, RET_DK)
    log_g = jnp.log1p(-jnp.exp2(-5.0 - jnp.arange(RET_HEADS, dtype=F32)))
    lg = jnp.broadcast_to(log_g.reshape(RET_HEADS // 2, 2, 1), (RET_HEADS // 2, 2, LANES))
    grp = jnp.arange(MXU_DIM) // DIFF_DH
    gm = ((grp[:, None] == grp[None, :]).astype(F32) / DIFF_DH).astype(BF16)
    ones_row = jnp.ones((1, LANES), F32)

    x2 = x.reshape(m, d_model)
    for i in range(depth):
        proj = _inproj(x2, norm1_g[i][None, :], _permute_w_in(w_in[i]), t.rows_big, N_PROJ // 3)

        wd, bd = _lru_weights(lru_wa[i], lru_ba[i], lru_wx[i], lru_bx[i])
        y_a = _rglru(proj, conv_w[i], conv_b[i][None, :], wd, bd, lru_lambda[i], batch, seq)

        lam_init = 0.8 - 0.6 * math.exp(-0.3 * i)
        lp = diff_lam[i].astype(F32)
        lam = jnp.exp(jnp.sum(lp[0] * lp[1])) - jnp.exp(jnp.sum(lp[2] * lp[3])) + lam_init
        gq = jnp.tile(diff_q_g[i], MXU_DIM // DIFF_DH)[None, :] * (DIFF_DH ** -0.5 * LOG2E)
        gk = jnp.tile(diff_k_g[i], MXU_DIM // DIFF_DH)[None, :]
        qb, kb = _diff_prep(proj, gq, gk, tabs_b, gm, rp_b, seq, t.rows_small)
        attn_b = functools.partial(_attention, v_off=OFF_BV, batch=batch, seq=seq, heads=DIFF_HEADS,
                                   dq=LANES, tq=t.q_diff, tk=t.keys, diff=True,
                                   out_scale=1.0 - lam_init)
        y_b = lax.cond(
            _score_bound(diff_q_g[i], diff_k_g[i], DIFF_DH) <= MAX_SCORE_LOG2,
            functools.partial(attn_b, bounded=True), functools.partial(attn_b, bounded=False),
            qb, kb, proj, jnp.full((1, LANES), lam, F32), diff_sub_g[i][None, :])

        wq, wkv, gqp, gkp = _mla_weights(mla_wuq[i], mla_wukv[i], mla_q_g[i], mla_k_g[i])
        qc, kc, vc = _mla_prep(proj, mla_qa_g[i][None, :], wq, mla_kva_g[i][None, :], wkv,
                               gqp, gkp, tabs_c, rp_c, seq, t.rows_small,
                               (MLA_NOPE + MLA_ROPE) ** -0.5 * LOG2E)
        attn_c = functools.partial(_attention, v_off=0, batch=batch, seq=seq, heads=MLA_HEADS,
                                   dq=MLA_HEAD_PAD, tq=t.q_mla, tk=t.keys, diff=False,
                                   out_scale=1.0)
        y_c = lax.cond(
            _score_bound(mla_q_g[i], mla_k_g[i], MLA_NOPE + MLA_ROPE) <= MAX_SCORE_LOG2,
            functools.partial(attn_c, bounded=True), functools.partial(attn_c, bounded=False),
            qc, kc, vc, ones_row, ones_row)

        y_d = _retention(proj, lg, ret_gn_g[i][None, :], tabs_d, rp_d, batch, seq)

        x2 = _merge(x2, proj, gate_b[i].reshape(1, N_BRANCH * D_MODEL), (y_a, y_b, y_c, y_d),
                    tuple(w[i].astype(BF16) for w in (w_br_a, w_br_b, w_br_c, w_br_d)),
                    w_out[i].astype(BF16), t.rows_small)

        x2 = _ffn(x2, norm2_g[i][None, :], w_ff1[i].astype(BF16), w_ff2[i].astype(BF16),
                  norm3_g[i][None, :], w_ple_gate[i].astype(BF16),
                  p.reshape(depth * m, p.shape[-1]), i, w_ple_proj[i].astype(BF16), t.rows_big, t.ffn_cols)
    return x2.reshape(batch, seq, d_model)
```
